```python
import jax, jax.numpy as jnp
from jax import lax
import numpy as np

D_MODEL = 1024
BATCH = 2
SEQ = 8192
DEPTH = 4

HEAD_DIM = 64
NORM_EPS = 1e-6
RWKV_HEADS = 8
RWKV_WIDTH = RWKV_HEADS * HEAD_DIM
LORA_DECAY = 64
LORA_AAA = 64
LORA_GATE = 128
RWKV_COLS = 3 * RWKV_WIDTH + LORA_DECAY + LORA_AAA + LORA_GATE
RWKV_SPLITS = (RWKV_WIDTH, 2 * RWKV_WIDTH, 3 * RWKV_WIDTH,
               3 * RWKV_WIDTH + LORA_DECAY, 3 * RWKV_WIDTH + LORA_DECAY + LORA_AAA)
RWKV_GN_EPS = 64e-5
CONV_WIDTH = 512
CONV_K = 3
CONV_COLS = 3 * CONV_WIDTH
ATTN_GROUPS = ((128, 1), (512, 4), (2048, 16))
HEADS_PER_GROUP = 4
ATTN_HEADS = HEADS_PER_GROUP * len(ATTN_GROUPS)
ATTN_WIDTH = ATTN_HEADS * HEAD_DIM
ATTN_COLS = 3 * ATTN_WIDTH
ATTN_OUT_WIDTH = HEADS_PER_GROUP * HEAD_DIM
ALIBI_MAX_EXP = 8.0
N_BRANCH = 3
GATE_COLS = N_BRANCH * D_MODEL
IN_COLS = RWKV_COLS + CONV_COLS + ATTN_COLS + GATE_COLS
IN_SPLITS = (RWKV_COLS, RWKV_COLS + CONV_COLS, RWKV_COLS + CONV_COLS + ATTN_COLS)
BRANCH_ROWS = RWKV_WIDTH + CONV_WIDTH + ATTN_OUT_WIDTH
D_FF = ((8 * D_MODEL + 767) // 768) * 256

kernel_name = "hybrid_rwkv7_shortconv_dilatedattn_gated"


def rmsnorm(x, g):
    xf = x.astype(jnp.float32)
    y = xf * lax.rsqrt(jnp.mean(xf * xf, axis=-1, keepdims=True) + NORM_EPS)
    return (y * g.astype(jnp.float32)).astype(x.dtype)


def token_shift(t):
    return jnp.pad(t, ((0, 0), (1, 0), (0, 0)))[:, :-1]


def rwkv7_step(state, inp):
    r, w, k, v, a, b = inp
    sa = jnp.einsum('bhvk,bhk->bhv', state, a)
    state = state * w[:, :, None, :] + sa[..., None] * b[:, :, None, :] + v[..., None] * k[:, :, None, :]
    y = jnp.einsum('bhvk,bhk->bhv', state, r)
    return state, y


def rwkv7_time_mix(p, mu, w0, w_up, a0, a_up, g_up, k_k, k_a, r_k, ln_w, ln_b):
    bsz, seq, _ = p.shape
    xm = p + (token_shift(p) - p) * mu
    r, k, v, wl, al, gl = jnp.split(xm, RWKV_SPLITS, axis=-1)
    w_log = -jax.nn.softplus(-(w0 + jnp.tanh(wl) @ w_up)) - 0.5
    a = jax.nn.sigmoid(a0 + al @ a_up)
    g = jax.nn.sigmoid(gl) @ g_up

    def heads(t):
        return t.astype(jnp.float32).reshape(bsz, seq, RWKV_HEADS, HEAD_DIM)

    kk = heads(k * k_k)
    kk = kk * lax.rsqrt(jnp.maximum(jnp.sum(kk * kk, axis=-1, keepdims=True), 1e-12))
    k = k * (1 + (a - 1) * k_a)
    rh, kh, vh, ah = heads(r), heads(k), heads(v), heads(a)
    decay = jnp.exp(-jnp.exp(heads(w_log)))
    xs = tuple(jnp.moveaxis(t, 1, 0) for t in (rh, decay, kh, vh, -kk, kk * ah))
    state0 = jnp.zeros((bsz, RWKV_HEADS, HEAD_DIM, HEAD_DIM), jnp.float32)
    _, y = lax.scan(rwkv7_step, state0, xs)
    y = jnp.moveaxis(y, 0, 1)
    mean = jnp.mean(y, axis=-1, keepdims=True)
    var = jnp.mean(jnp.square(y - mean), axis=-1, keepdims=True)
    y = (y - mean) * lax.rsqrt(var + RWKV_GN_EPS)
    y = y * ln_w.astype(jnp.float32).reshape(RWKV_HEADS, HEAD_DIM) + ln_b.astype(jnp.float32).reshape(RWKV_HEADS, HEAD_DIM)
    y = y + jnp.sum(rh * kh * r_k.astype(jnp.float32), axis=-1, keepdims=True) * vh
    return (y.reshape(bsz, seq, RWKV_WIDTH) * g.astype(jnp.float32)).astype(p.dtype)


def short_conv_mix(p, conv_w):
    seq = p.shape[1]
    b_gate, c_gate, xc = jnp.split(p, 3, axis=-1)
    u = jnp.pad(c_gate * xc, ((0, 0), (CONV_K - 1, 0), (0, 0)))
    y = conv_w[0] * u[:, 0:seq] + conv_w[1] * u[:, 1:seq + 1] + conv_w[2] * u[:, 2:seq + 2]
    return b_gate * y


def alibi_slopes():
    return jnp.exp2(-ALIBI_MAX_EXP * jnp.arange(1, ATTN_HEADS + 1, dtype=jnp.float32) / ATTN_HEADS)


def dilated_window_group(q, k, v, slopes, window, dilation):
    bsz, seq, hg, e = q.shape
    blk = window // dilation
    sub_len = seq // dilation
    n_blk = -(-sub_len // blk)
    pad = n_blk * blk - sub_len

    def to_blocks(t):
        t = t.reshape(bsz, sub_len, dilation, hg, e).transpose(0, 2, 1, 3, 4)
        t = jnp.pad(t, ((0, 0), (0, 0), (0, pad), (0, 0), (0, 0)))
        return t.reshape(bsz, dilation, n_blk, blk, hg, e)

    def with_prev(t):
        prev = jnp.pad(t, ((0, 0), (0, 0), (1, 0), (0, 0), (0, 0), (0, 0)))[:, :, :-1]
        return jnp.concatenate([prev, t], axis=3)

    qb = to_blocks(q)
    kc = with_prev(to_blocks(k))
    vc = with_prev(to_blocks(v))
    s = jnp.einsum('bdnqhe,bdnkhe->bdnhqk', qb, kc) * (HEAD_DIM ** -0.5)
    steps = blk + jnp.arange(blk)[:, None] - jnp.arange(2 * blk)[None, :]
    in_window = (steps >= 0) & (steps <= blk)
    exists = (jnp.arange(n_blk)[:, None] > 0) | (jnp.arange(2 * blk)[None, :] >= blk)
    mask = in_window[None] & exists[:, None, :]
    alibi = -slopes[:, None, None] * (steps * dilation).astype(jnp.float32)[None]
    s = jnp.where(mask[None, None, :, None], s + alibi[None, None, None], -jnp.inf)
    m = jnp.max(s, axis=-1, keepdims=True)
    pexp = jnp.exp(s - m)
    l = jnp.sum(pexp, axis=-1, keepdims=True)
    o = jnp.einsum('bdnhqk,bdnkhe->bdnqhe', pexp / l, vc)
    lse = jnp.transpose((m + jnp.log(l))[..., 0], (0, 1, 2, 4, 3))

    def from_blocks(t):
        t = t.reshape((bsz, dilation, n_blk * blk) + t.shape[4:])[:, :, :sub_len]
        t = jnp.moveaxis(t, 1, 2)
        return t.reshape((bsz, seq) + t.shape[3:])

    return from_blocks(o), from_blocks(lse)


def dilated_attention_mix(p):
    bsz, seq, _ = p.shape
    q, k, v = (t.astype(jnp.float32).reshape(bsz, seq, ATTN_HEADS, HEAD_DIM) for t in jnp.split(p, 3, axis=-1))
    slopes = alibi_slopes()
    outs, lses = [], []
    for gi, (window, dilation) in enumerate(ATTN_GROUPS):
        sl = slice(gi * HEADS_PER_GROUP, (gi + 1) * HEADS_PER_GROUP)
        o, lse = dilated_window_group(q[:, :, sl], k[:, :, sl], v[:, :, sl], slopes[sl], window, dilation)
        outs.append(o)
        lses.append(lse)
    o = jnp.stack(outs)
    wts = jax.nn.softmax(jnp.stack(lses), axis=0)
    y = jnp.sum(wts[..., None] * o, axis=0)
    return y.reshape(bsz, seq, ATTN_OUT_WIDTH).astype(p.dtype)


def setup_inputs(seed: int = 0) -> dict:
    key = jax.random.key(seed)
    ks = jax.random.split(key, 32)
    f32 = jnp.float32

    def nrm(k, shape, scale):
        return jax.random.normal(k, shape, f32) * scale

    def gain(k, shape):
        return 1.0 + 0.05 * jax.random.normal(k, shape, f32)

    w_branch = jnp.concatenate([
        nrm(ks[20], (DEPTH, RWKV_WIDTH, D_MODEL), RWKV_WIDTH ** -0.5),
        nrm(ks[21], (DEPTH, CONV_WIDTH, D_MODEL), CONV_WIDTH ** -0.5),
        nrm(ks[22], (DEPTH, ATTN_OUT_WIDTH, D_MODEL), ATTN_OUT_WIDTH ** -0.5)], axis=1)
    return {
        "x": nrm(ks[0], (BATCH, SEQ, D_MODEL), 1.0),
        "norm_mix_pre": gain(ks[1], (DEPTH, D_MODEL)),
        "norm_mix_post": gain(ks[2], (DEPTH, D_MODEL)),
        "norm_ffn_pre": gain(ks[3], (DEPTH, D_MODEL)),
        "norm_ffn_post": gain(ks[4], (DEPTH, D_MODEL)),
        "w_in": nrm(ks[5], (DEPTH, D_MODEL, IN_COLS), D_MODEL ** -0.5),
        "rwkv_mu": jax.random.uniform(ks[6], (DEPTH, RWKV_COLS), f32),
        "rwkv_w0": jax.random.uniform(ks[7], (DEPTH, RWKV_WIDTH), f32, -6.0, 1.0),
        "rwkv_w_up": nrm(ks[8], (DEPTH, LORA_DECAY, RWKV_WIDTH), 0.5 * LORA_DECAY ** -0.5),
        "rwkv_a0": nrm(ks[9], (DEPTH, RWKV_WIDTH), 0.5),
        "rwkv_a_up": nrm(ks[10], (DEPTH, LORA_AAA, RWKV_WIDTH), 0.5 * LORA_AAA ** -0.5),
        "rwkv_g_up": nrm(ks[11], (DEPTH, LORA_GATE, RWKV_WIDTH), LORA_GATE ** -0.5),
        "rwkv_k_k": 0.85 + 0.05 * jax.random.normal(ks[12], (DEPTH, RWKV_WIDTH), f32),
        "rwkv_k_a": gain(ks[13], (DEPTH, RWKV_WIDTH)),
        "rwkv_r_k": nrm(ks[14], (DEPTH, RWKV_HEADS, HEAD_DIM), 0.1),
        "rwkv_ln_w": gain(ks[15], (DEPTH, RWKV_WIDTH)),
        "rwkv_ln_b": nrm(ks[16], (DEPTH, RWKV_WIDTH), 0.02),
        "conv_w": nrm(ks[17], (DEPTH, CONV_K, CONV_WIDTH), CONV_K ** -0.5),
        "w_branch": w_branch,
        "w_out": nrm(ks[23], (DEPTH, D_MODEL, D_MODEL), D_MODEL ** -0.5),
        "w_ffn_in": nrm(ks[24], (DEPTH, D_MODEL, 2 * D_FF), D_MODEL ** -0.5),
        "w_ffn_out": nrm(ks[25], (DEPTH, D_FF, D_MODEL), D_FF ** -0.5),
    }


def reference(x, norm_mix_pre, norm_mix_post, norm_ffn_pre, norm_ffn_post, w_in, rwkv_mu, rwkv_w0,
              rwkv_w_up, rwkv_a0, rwkv_a_up, rwkv_g_up, rwkv_k_k, rwkv_k_a, rwkv_r_k, rwkv_ln_w, rwkv_ln_b,
              conv_w, w_branch, w_out, w_ffn_in, w_ffn_out):
    for l in range(DEPTH):
        h = rmsnorm(x, norm_mix_pre[l])
        proj = h @ w_in[l]
        p_rwkv, p_conv, p_attn, p_gate = jnp.split(proj, IN_SPLITS, axis=-1)
        y_a = rwkv7_time_mix(p_rwkv, rwkv_mu[l], rwkv_w0[l], rwkv_w_up[l], rwkv_a0[l], rwkv_a_up[l],
                             rwkv_g_up[l], rwkv_k_k[l], rwkv_k_a[l], rwkv_r_k[l], rwkv_ln_w[l], rwkv_ln_b[l])
        y_b = short_conv_mix(p_conv, conv_w[l])
        y_c = dilated_attention_mix(p_attn)
        g_a, g_b, g_c = jnp.split(jax.nn.sigmoid(p_gate), N_BRANCH, axis=-1)
        wb = w_branch[l]
        merged = (g_a * (y_a @ wb[:RWKV_WIDTH])
                  + g_b * (y_b @ wb[RWKV_WIDTH:RWKV_WIDTH + CONV_WIDTH])
                  + g_c * (y_c @ wb[RWKV_WIDTH + CONV_WIDTH:]))
        x = x + rmsnorm(merged @ w_out[l], norm_mix_post[l])
        h = rmsnorm(x, norm_ffn_pre[l])
        gate, up = jnp.split(h @ w_ffn_in[l], 2, axis=-1)
        x = x + rmsnorm((jax.nn.silu(gate) * up) @ w_ffn_out[l], norm_ffn_post[l])
    return x
```

```python
import functools

import jax
import jax.numpy as jnp
from jax import lax
from jax.experimental import pallas as pl
from jax.experimental.pallas import tpu as pltpu

F32 = jnp.float32
BF16 = jnp.bfloat16

HEAD_DIM = 64
NORM_EPS = 1e-6
RWKV_HEADS = 8
RWKV_WIDTH = RWKV_HEADS * HEAD_DIM
LORA_DECAY = 64
LORA_AAA = 64
LORA_GATE = 128
RWKV_COLS = 3 * RWKV_WIDTH + LORA_DECAY + LORA_AAA + LORA_GATE
RWKV_GN_EPS = 64e-5
CONV_WIDTH = 512
CONV_COLS = 3 * CONV_WIDTH
ATTN_GROUPS = ((128, 1), (512, 4), (2048, 16))
HEADS_PER_GROUP = 4
ATTN_HEADS = HEADS_PER_GROUP * len(ATTN_GROUPS)
ATTN_WIDTH = ATTN_HEADS * HEAD_DIM
ATTN_COLS = 3 * ATTN_WIDTH
ATTN_OUT_WIDTH = HEADS_PER_GROUP * HEAD_DIM
ALIBI_MAX_EXP = 8.0
ATTN_BLK = 128
MASK_VALUE = -1e30

CHUNK = 64
TOKEN_TILE = 512
VMEM_LIMIT = 56 * 1024 * 1024


def _params(sem):
    return pltpu.CompilerParams(dimension_semantics=sem, vmem_limit_bytes=VMEM_LIMIT)


def _resident(shape):
    zeros = (0,) * len(shape)
    return pl.BlockSpec(shape, lambda *_: zeros, pipeline_mode=pl.Buffered(1))


def _dot(a, b):
    return jnp.dot(a.astype(BF16), b.astype(BF16), preferred_element_type=F32)


def _dot_nt(a, b):
    return lax.dot_general(a.astype(BF16), b.astype(BF16), (((1,), (1,)), ((), ())), preferred_element_type=F32)


def _dot_tn(a, b):
    return lax.dot_general(a.astype(BF16), b.astype(BF16), (((0,), (0,)), ((), ())), preferred_element_type=F32)


def _split3(x):
    hi = x.astype(BF16)
    r1 = x - hi.astype(F32)
    mid = r1.astype(BF16)
    lo = (r1 - mid.astype(F32)).astype(BF16)
    return hi, mid, lo


def _dot_f32_lhs(a, b01):
    hi, mid, lo = _split3(a)
    d = lambda x: jnp.dot(x, b01, preferred_element_type=F32)
    return d(hi) + d(mid) + d(lo)


def _dot_f32_rhs(a01, b):
    hi, mid, lo = _split3(b)
    d = lambda x: jnp.dot(a01, x, preferred_element_type=F32)
    return d(hi) + d(mid) + d(lo)


def _dot_x3(a, b):
    ah = a.astype(BF16)
    al = (a - ah.astype(F32)).astype(BF16)
    bh = b.astype(BF16)
    bl = (b - bh.astype(F32)).astype(BF16)
    d = lambda x, y: jnp.dot(x, y, preferred_element_type=F32)
    return d(ah, bh) + d(ah, bl) + d(al, bh)


def _rms(x, gain):
    return x * lax.rsqrt(jnp.mean(x * x, axis=-1, keepdims=True) + NORM_EPS) * gain


def _sigmoid(z):
    return 1.0 / (1.0 + jnp.exp(-z))


def _softplus(z):
    return jnp.maximum(z, 0.0) + jnp.log(1.0 + jnp.exp(-jnp.abs(z)))


def _shift_rows(cur, prev_tail, n):
    rows = lax.broadcasted_iota(jnp.int32, cur.shape, 0)
    out = pltpu.roll(cur, n, axis=0)
    for i in range(n):
        out = jnp.where(rows == i, prev_tail[8 - n + i:8 - n + i + 1, :], out)
    return out


def _inproj_kernel(x_ref, gain_ref, w_ref, o_rwkv, o_conv, o_attn, o_gate):
    h = _rms(x_ref[...], gain_ref[...]).astype(BF16)
    lo = 0
    for o_ref in (o_rwkv, o_conv, o_attn, o_gate):
        width = o_ref.shape[-1]
        for c in range(0, width, 256):
            o_ref[:, c:c + 256] = jnp.dot(h, w_ref[:, lo + c:lo + c + 256],
                                          preferred_element_type=F32).astype(o_ref.dtype)
        lo += width


def _inproj(x2d, gain, w_in_bf16):
    m, d = x2d.shape
    tm = TOKEN_TILE
    widths = (RWKV_COLS, CONV_COLS, ATTN_COLS, w_in_bf16.shape[1] - RWKV_COLS - CONV_COLS - ATTN_COLS)
    dtypes = (F32, BF16, BF16, BF16)
    return pl.pallas_call(
        _inproj_kernel,
        grid=(m // tm,),
        in_specs=[pl.BlockSpec((tm, d), lambda i: (i, 0)),
                  _resident((1, d)),
                  _resident(w_in_bf16.shape)],
        out_specs=[pl.BlockSpec((tm, w), lambda i: (i, 0)) for w in widths],
        out_shape=[jax.ShapeDtypeStruct((m, w), dt) for w, dt in zip(widths, dtypes)],
        compiler_params=_params(("parallel",)),
        name="inproj",
    )(x2d, gain.reshape(1, d), w_in_bf16)


def _rwkv_prep_kernel(p_ref, prev_ref, mu_ref, w0_ref, wup_ref, a0_ref, aup_ref, gup_ref, kk_ref, ka_ref,
                      rk_ref, bd_ref, ltri_ref, lsum_ref, sel_ref,
                      rt_ref, at_ref, bt_ref, kt_ref, bh_ref, kh_ref, v_ref, gam_ref, g_ref, bonus_ref):
    first = pl.program_id(1) == 0
    p = p_ref[0]
    prev = jnp.where(first, 0.0, prev_ref[0])
    xm = p + (_shift_rows(p, prev, 1) - p) * mu_ref[...]
    w = RWKV_WIDTH
    r, k, v = xm[:, 0:w], xm[:, w:2 * w], xm[:, 2 * w:3 * w]
    wa = xm[:, 3 * w:3 * w + LORA_DECAY + LORA_AAA]
    gl = xm[:, 3 * w + LORA_DECAY + LORA_AAA:]
    w_log = -_softplus(-(w0_ref[...] + _dot_x3(jnp.tanh(wa), wup_ref[...]))) - 0.5
    lw = -jnp.exp(w_log)
    a_lr = _sigmoid(a0_ref[...] + _dot_x3(wa, aup_ref[...]))
    g = _dot(_sigmoid(gl), gup_ref[...])
    bd = bd_ref[...]
    kk = k * kk_ref[...]
    kk = kk * lax.rsqrt(jnp.maximum(_dot_f32_lhs(kk * kk, bd), 1e-12))
    k2 = k * (1.0 + (a_lr - 1.0) * ka_ref[...])
    bonus = _dot_f32_lhs(r * k2 * rk_ref[...], bd) * v
    a = -kk
    b = kk * a_lr
    cum = _dot_f32_rhs(ltri_ref[...], lw)
    tot = _dot_f32_rhs(lsum_ref[...], lw)
    e_neg = jnp.exp(-cum)
    e_rem = jnp.exp(tot - cum)
    rt_ref[0] = (r * jnp.exp(cum)).astype(BF16)
    at_ref[0] = (a * jnp.exp(cum - lw)).astype(BF16)
    bt_ref[0] = (b * e_neg).astype(BF16)
    kt_ref[0] = (k2 * e_neg).astype(BF16)
    bh_ref[0] = (b * e_rem).astype(BF16)
    kh_ref[0] = (k2 * e_rem).astype(BF16)
    v_ref[0] = v.astype(BF16)
    gam_ref[0] = jnp.exp(_dot_f32_rhs(sel_ref[...], lw))
    g_ref[0] = g
    bonus_ref[0] = bonus


def _rwkv_prep(p3, mu, w0, w_up, a0, a_up, g_up, k_k, k_a, r_k):
    bsz, seq, cols = p3.shape
    tm = TOKEN_TILE
    w = RWKV_WIDTH
    nck = tm // CHUNK
    lora = LORA_DECAY + LORA_AAA
    wup_pad = jnp.zeros((lora, w), F32).at[:LORA_DECAY].set(w_up)
    aup_pad = jnp.zeros((lora, w), F32).at[LORA_DECAY:].set(a_up)
    hid = jnp.arange(w) // HEAD_DIM
    bd = (hid[:, None] == hid[None, :]).astype(BF16)
    t = jnp.arange(tm)
    same = (t[:, None] // CHUNK) == (t[None, :] // CHUNK)
    ltri = (same & (t[:, None] >= t[None, :])).astype(BF16)
    lsum = same.astype(BF16)
    sel = (jnp.arange(nck)[:, None] == (t[None, :] // CHUNK)).astype(BF16)
    row = lambda vec: vec.reshape(1, -1)
    tok = lambda width: pl.BlockSpec((1, tm, width), lambda b, j: (b, j, 0))
    ins = [tok(cols),
           pl.BlockSpec((1, 8, cols), lambda b, j: (b, jnp.maximum(j * (tm // 8) - 1, 0), 0)),
           _resident((1, cols)), _resident((1, w)), _resident((lora, w)), _resident((1, w)),
           _resident((lora, w)), _resident((LORA_GATE, w)), _resident((1, w)), _resident((1, w)),
           _resident((1, w)), _resident((w, w)), _resident((tm, tm)), _resident((tm, tm)),
           _resident((nck, tm))]
    outs = [tok(w)] * 7 + [pl.BlockSpec((1, nck, w), lambda b, j: (b, j, 0)), tok(w), tok(w)]
    shapes = ([jax.ShapeDtypeStruct((bsz, seq, w), BF16)] * 7
              + [jax.ShapeDtypeStruct((bsz, seq // CHUNK, w), F32)]
              + [jax.ShapeDtypeStruct((bsz, seq, w), F32)] * 2)
    return pl.pallas_call(
        _rwkv_prep_kernel,
        grid=(bsz, seq // tm),
        in_specs=ins, out_specs=outs, out_shape=shapes,
        compiler_params=_params(("parallel", "parallel")),
        name="rwkv_prep",
    )(p3, p3, row(mu), row(w0), wup_pad, row(a0), aup_pad, g_up.astype(BF16), row(k_k), row(k_a),
      row(r_k), bd, ltri, lsum, sel)


def _rwkv_scan_kernel(rt_ref, at_ref, bt_ref, kt_ref, bh_ref, kh_ref, v_ref, gam_ref, g_ref, bonus_ref,
                      lnw_ref, lnb_ref, bd_ref, o_ref, s_scr, y_scr):
    @pl.when(pl.program_id(1) == 0)
    def _():
        s_scr[...] = jnp.zeros_like(s_scr)

    c2 = 2 * CHUNK
    ii = lax.broadcasted_iota(jnp.int32, (c2, CHUNK), 0)
    jj = lax.broadcasted_iota(jnp.int32, (c2, CHUNK), 1)
    tri = jj < jnp.where(ii < CHUNK, ii, ii - (CHUNK - 1))
    eye = (lax.broadcasted_iota(jnp.int32, (CHUNK, CHUNK), 0)
           == lax.broadcasted_iota(jnp.int32, (CHUNK, CHUNK), 1)).astype(F32)

    def chunk_body(c, carry):
        t0 = pl.multiple_of(c * CHUNK, CHUNK)
        rows = pl.ds(t0, CHUNK)
        gam_all = gam_ref[0, pl.ds(c, 1), :]
        for h in range(RWKV_HEADS):
            cs = slice(h * HEAD_DIM, (h + 1) * HEAD_DIM)
            rt, at, bt, kt = rt_ref[0, rows, cs], at_ref[0, rows, cs], bt_ref[0, rows, cs], kt_ref[0, rows, cs]
            bh, kh, v = bh_ref[0, rows, cs], kh_ref[0, rows, cs], v_ref[0, rows, cs]
            x = jnp.concatenate([at, rt], axis=0)
            ab = jnp.where(tri, _dot_nt(x, bt), 0.0)
            ak = jnp.where(tri, _dot_nt(x, kt), 0.0)
            a_ab, a_rb = ab[:CHUNK], ab[CHUNK:].astype(BF16)
            a_ak, a_rk = ak[:CHUNK], ak[CHUNK:]
            pw = a_ab
            t_inv = eye + pw
            for _ in range(5):
                pw = _dot(pw, pw)
                t_inv = t_inv + _dot(t_inv, pw)
            t_inv = t_inv.astype(BF16)
            u = _dot(t_inv, _dot(a_ak, v))
            wm = _dot(t_inv, at)
            qh = rt.astype(F32) + _dot(a_rb, wm)
            yi = _dot(a_rb, u) + _dot(a_rk, v)
            mp = _dot_tn(wm, bh)
            npart = _dot_tn(u, bh) + _dot_tn(v, kh)
            s0 = s_scr[h]
            y_scr[rows, cs] = _dot_nt(qh, s0) + yi
            s_scr[h] = s0 * gam_all[:, cs] + _dot(s0, mp) + npart
        return carry

    lax.fori_loop(0, rt_ref.shape[1] // CHUNK, chunk_body, 0)

    y = y_scr[...]
    bd = bd_ref[...]
    inv_n = 1.0 / HEAD_DIM
    mean = _dot_f32_lhs(y, bd) * inv_n
    d = y - mean
    var = _dot_f32_lhs(d * d, bd) * inv_n
    yn = d * lax.rsqrt(var + RWKV_GN_EPS) * lnw_ref[...] + lnb_ref[...]
    o_ref[0] = ((yn + bonus_ref[0]) * g_ref[0]).astype(o_ref.dtype)


def _rwkv_scan(prep, ln_w, ln_b):
    rt = prep[0]
    bsz, seq, w = rt.shape
    tm = TOKEN_TILE
    nck = tm // CHUNK
    hid = jnp.arange(w) // HEAD_DIM
    bd = (hid[:, None] == hid[None, :]).astype(BF16)
    tok = pl.BlockSpec((1, tm, w), lambda b, j: (b, j, 0))
    ins = [tok] * 7 + [pl.BlockSpec((1, nck, w), lambda b, j: (b, j, 0)), tok, tok,
                       _resident((1, w)), _resident((1, w)), _resident((w, w))]
    return pl.pallas_call(
        _rwkv_scan_kernel,
        grid=(bsz, seq // tm),
        in_specs=ins, out_specs=tok,
        out_shape=jax.ShapeDtypeStruct((bsz, seq, w), BF16),
        scratch_shapes=[pltpu.VMEM((RWKV_HEADS, HEAD_DIM, HEAD_DIM), F32), pltpu.VMEM((tm, w), F32)],
        compiler_params=_params(("parallel", "arbitrary")),
        name="rwkv_scan",
    )(*prep, ln_w.reshape(1, w), ln_b.reshape(1, w), bd)


def _attn_kernel(q_ref, kp_ref, kc_ref, vp_ref, vc_ref, bias_ref, o_ref, l_ref):
    blk = ATTN_BLK
    has_prev = pl.program_id(2) > 0
    q = q_ref[0]
    kcat = jnp.concatenate([kp_ref[0], kc_ref[0]], axis=0)
    vcat = jnp.concatenate([vp_ref[0], vc_ref[0]], axis=0)
    col = lax.broadcasted_iota(jnp.int32, (blk, 2 * blk), 1)
    exists = jnp.logical_or(has_prev, col >= blk)
    lane = lax.broadcasted_iota(jnp.int32, (blk, 2 * HEAD_DIM), 1)
    for pair in range(HEADS_PER_GROUP // 2):
        ls = slice(pair * 2 * HEAD_DIM, (pair + 1) * 2 * HEAD_DIM)
        qp, kp, vp = q[:, ls], kcat[:, ls], vcat[:, ls]
        o_pair = None
        l_pair = None
        for hh in range(2):
            mine = (lane // HEAD_DIM) == hh
            s = _dot_nt(jnp.where(mine, qp, jnp.zeros_like(qp)), kp) * (HEAD_DIM ** -0.5)
            s = jnp.where(exists, s + bias_ref[pair * 2 + hh], MASK_VALUE)
            m = jnp.max(s, axis=-1, keepdims=True)
            pexp = jnp.exp(s - m)
            l = jnp.sum(pexp, axis=-1, keepdims=True)
            o = _dot(pexp, vp) / l
            lse = jnp.broadcast_to(m + jnp.log(l), o.shape)
            o_pair = o if o_pair is None else jnp.where(mine, o, o_pair)
            l_pair = lse if l_pair is None else jnp.where(mine, lse, l_pair)
        o_ref[0, :, ls] = o_pair
        l_ref[0, :, ls] = l_pair


def _attn_bias(group, dilation):
    blk = ATTN_BLK
    heads = jnp.arange(group * HEADS_PER_GROUP + 1, (group + 1) * HEADS_PER_GROUP + 1, dtype=F32)
    slopes = jnp.exp2(-ALIBI_MAX_EXP * heads / ATTN_HEADS)
    steps = blk + jnp.arange(blk)[:, None] - jnp.arange(2 * blk)[None, :]
    in_window = (steps >= 0) & (steps <= blk)
    alibi = -slopes[:, None, None] * (steps * dilation).astype(F32)[None]
    return jnp.where(in_window[None], alibi, MASK_VALUE)


def _attn_group(p_attn3, group, dilation):
    bsz, seq, cols = p_attn3.shape
    blk = ATTN_BLK
    gw = ATTN_OUT_WIDTH
    sub = seq // dilation
    nblk = sub // blk
    ncb = cols // gw
    pv = p_attn3.reshape(bsz, sub, dilation * cols)
    heads_blocks = ATTN_WIDTH // gw

    def spec(which, prev):
        cb = which * heads_blocks + group
        if prev:
            return pl.BlockSpec((1, blk, gw), lambda b, r, n: (b, jnp.maximum(n - 1, 0), r * ncb + cb))
        return pl.BlockSpec((1, blk, gw), lambda b, r, n: (b, n, r * ncb + cb))

    out_spec = pl.BlockSpec((1, blk, gw), lambda b, r, n: (b, n, r))
    o, lse = pl.pallas_call(
        _attn_kernel,
        grid=(bsz, dilation, nblk),
        in_specs=[spec(0, False), spec(1, True), spec(1, False), spec(2, True), spec(2, False),
                  _resident((HEADS_PER_GROUP, blk, 2 * blk))],
        out_specs=[out_spec, out_spec],
        out_shape=[jax.ShapeDtypeStruct((bsz, sub, dilation * gw), F32)] * 2,
        compiler_params=_params(("parallel", "parallel", "arbitrary")),
        name=f"attn_g{group}",
    )(pv, pv, pv, pv, pv, _attn_bias(group, dilation))
    return o.reshape(bsz, seq, gw), lse.reshape(bsz, seq, gw)


def _merge_kernel(x_ref, ya_ref, pc_ref, pcprev_ref, o0_ref, o1_ref, o2_ref, l0_ref, l1_ref, l2_ref,
                  gate_ref, convw_ref, wb_ref, wout_ref, gain_ref, out_ref):
    first = pl.program_id(1) == 0
    cw = CONV_WIDTH
    pc = pc_ref[0].astype(F32)
    pprev = pcprev_ref[0].astype(F32)
    u = pc[:, cw:2 * cw] * pc[:, 2 * cw:]
    uprev = jnp.where(first, 0.0, pprev[:, cw:2 * cw] * pprev[:, 2 * cw:])
    cwt = convw_ref[...]
    yb = pc[:, :cw] * (cwt[0:1] * _shift_rows(u, uprev, 2) + cwt[1:2] * _shift_rows(u, uprev, 1) + cwt[2:3] * u)

    l0, l1, l2 = l0_ref[0], l1_ref[0], l2_ref[0]
    mx = jnp.maximum(jnp.maximum(l0, l1), l2)
    e0, e1, e2 = jnp.exp(l0 - mx), jnp.exp(l1 - mx), jnp.exp(l2 - mx)
    yc = (e0 * o0_ref[0] + e1 * o1_ref[0] + e2 * o2_ref[0]) / (e0 + e1 + e2)

    d = x_ref.shape[-1]
    gate = gate_ref[0].astype(F32)
    ra, rb = RWKV_WIDTH, RWKV_WIDTH + CONV_WIDTH
    merged = (_sigmoid(gate[:, :d]) * _dot(ya_ref[0], wb_ref[:ra])
              + _sigmoid(gate[:, d:2 * d]) * _dot(yb, wb_ref[ra:rb])
              + _sigmoid(gate[:, 2 * d:]) * _dot(yc, wb_ref[rb:]))
    out_ref[0] = x_ref[0] + _rms(_dot(merged, wout_ref[...]), gain_ref[...])


def _merge(x3, ya, p_conv3, attn, p_gate3, conv_w, wb_bf16, wout_bf16, gain):
    bsz, seq, d = x3.shape
    tm = TOKEN_TILE
    tok = lambda width: pl.BlockSpec((1, tm, width), lambda b, j: (b, j, 0))
    (o0, l0), (o1, l1), (o2, l2) = attn
    gw = ATTN_OUT_WIDTH
    ins = [tok(d), tok(RWKV_WIDTH), tok(CONV_COLS),
           pl.BlockSpec((1, 8, CONV_COLS), lambda b, j: (b, jnp.maximum(j * (tm // 8) - 1, 0), 0)),
           tok(gw), tok(gw), tok(gw), tok(gw), tok(gw), tok(gw), tok(p_gate3.shape[-1]),
           _resident(conv_w.shape), _resident(wb_bf16.shape), _resident(wout_bf16.shape), _resident((1, d))]
    return pl.pallas_call(
        _merge_kernel,
        grid=(bsz, seq // tm),
        in_specs=ins, out_specs=tok(d),
        out_shape=jax.ShapeDtypeStruct((bsz, seq, d), F32),
        compiler_params=_params(("parallel", "parallel")),
        name="merge",
    )(x3, ya, p_conv3, p_conv3, o0, o1, o2, l0, l1, l2, p_gate3, conv_w, wb_bf16, wout_bf16, gain.reshape(1, d))


def _ffn_kernel(x_ref, gpre_ref, gpost_ref, win_ref, wout_ref, out_ref, act_scr):
    x = x_ref[...]
    h = _rms(x, gpre_ref[...]).astype(BF16)
    dff = wout_ref.shape[0]
    for c in range(0, dff, 256):
        gate = jnp.dot(h, win_ref[:, c:c + 256], preferred_element_type=F32)
        up = jnp.dot(h, win_ref[:, dff + c:dff + c + 256], preferred_element_type=F32)
        act_scr[:, c:c + 256] = (gate * _sigmoid(gate) * up).astype(BF16)
    z = jnp.dot(act_scr[...], wout_ref[...], preferred_element_type=F32)
    out_ref[...] = x + _rms(z, gpost_ref[...])


def _ffn(x2d, gpre, gpost, win_bf16, wout_bf16):
    m, d = x2d.shape
    tm = TOKEN_TILE
    dff = wout_bf16.shape[0]
    return pl.pallas_call(
        _ffn_kernel,
        grid=(m // tm,),
        in_specs=[pl.BlockSpec((tm, d), lambda i: (i, 0)), _resident((1, d)), _resident((1, d)),
                  _resident(win_bf16.shape), _resident(wout_bf16.shape)],
        out_specs=pl.BlockSpec((tm, d), lambda i: (i, 0)),
        out_shape=jax.ShapeDtypeStruct((m, d), F32),
        scratch_shapes=[pltpu.VMEM((tm, dff), BF16)],
        compiler_params=_params(("parallel",)),
        name="ffn",
    )(x2d, gpre.reshape(1, d), gpost.reshape(1, d), win_bf16, wout_bf16)


def _layer(x3, norm_mix_pre, norm_mix_post, norm_ffn_pre, norm_ffn_post, w_in, rwkv_mu, rwkv_w0, rwkv_w_up,
           rwkv_a0, rwkv_a_up, rwkv_g_up, rwkv_k_k, rwkv_k_a, rwkv_r_k, rwkv_ln_w, rwkv_ln_b, conv_w,
           w_branch, w_out, w_ffn_in, w_ffn_out):
    bsz, seq, d = x3.shape
    m = bsz * seq
    p_rwkv, p_conv, p_attn, p_gate = _inproj(x3.reshape(m, d), norm_mix_pre, w_in.astype(BF16))
    in3 = lambda t: t.reshape(bsz, seq, t.shape[-1])
    prep = _rwkv_prep(in3(p_rwkv), rwkv_mu, rwkv_w0, rwkv_w_up, rwkv_a0, rwkv_a_up, rwkv_g_up,
                      rwkv_k_k, rwkv_k_a, rwkv_r_k.reshape(-1))
    ya = _rwkv_scan(prep, rwkv_ln_w, rwkv_ln_b)
    attn = [_attn_group(in3(p_attn), gi, dil) for gi, (_, dil) in enumerate(ATTN_GROUPS)]
    x3 = _merge(x3, ya, in3(p_conv), attn, in3(p_gate), conv_w, w_branch.astype(BF16), w_out.astype(BF16),
                norm_mix_post)
    x2 = _ffn(x3.reshape(m, d), norm_ffn_pre, norm_ffn_post, w_ffn_in.astype(BF16), w_ffn_out.astype(BF16))
    return x2.reshape(bsz, seq, d)


def kernel(x, norm_mix_pre, norm_mix_post, norm_ffn_pre, norm_ffn_post, w_in, rwkv_mu, rwkv_w0, rwkv_w_up, rwkv_a0, rwkv_a_up, rwkv_g_up, rwkv_k_k, rwkv_k_a, rwkv_r_k, rwkv_ln_w, rwkv_ln_b, conv_w, w_branch, w_out, w_ffn_in, w_ffn_out):
    params = (norm_mix_pre, norm_mix_post, norm_ffn_pre, norm_ffn_post, w_in, rwkv_mu, rwkv_w0, rwkv_w_up,
              rwkv_a0, rwkv_a_up, rwkv_g_up, rwkv_k_k, rwkv_k_a, rwkv_r_k, rwkv_ln_w, rwkv_ln_b, conv_w,
              w_branch, w_out, w_ffn_in, w_ffn_out)
    for layer in range(w_in.shape[0]):
        x = _layer(x, *(p[layer] for p in params))
    return x
```

```python
import functools

import jax
import jax.numpy as jnp
from jax import lax
from jax.experimental import pallas as pl
from jax.experimental.pallas import tpu as pltpu

F32 = jnp.float32
BF16 = jnp.bfloat16

HEAD_DIM = 64
NORM_EPS = 1e-6
RWKV_HEADS = 8
RWKV_WIDTH = RWKV_HEADS * HEAD_DIM
LORA_DECAY = 64
LORA_AAA = 64
LORA_GATE = 128
RWKV_COLS = 3 * RWKV_WIDTH + LORA_DECAY + LORA_AAA + LORA_GATE
RWKV_GN_EPS = 64e-5
CONV_WIDTH = 512
CONV_COLS = 3 * CONV_WIDTH
ATTN_GROUPS = ((128, 1), (512, 4), (2048, 16))
HEADS_PER_GROUP = 4
ATTN_HEADS = HEADS_PER_GROUP * len(ATTN_GROUPS)
ATTN_WIDTH = ATTN_HEADS * HEAD_DIM
ATTN_COLS = 3 * ATTN_WIDTH
ATTN_OUT_WIDTH = HEADS_PER_GROUP * HEAD_DIM
ALIBI_MAX_EXP = 8.0
ATTN_BLK = 128
MASK_VALUE = -1e30

CHUNK = 64
SUPER = 256
TOKEN_TILE = 512
VMEM_LIMIT = 56 * 1024 * 1024


def _params(sem):
    return pltpu.CompilerParams(dimension_semantics=sem, vmem_limit_bytes=VMEM_LIMIT)


def _resident(shape):
    zeros = (0,) * len(shape)
    return pl.BlockSpec(shape, lambda *_: zeros, pipeline_mode=pl.Buffered(1))


def _dot(a, b):
    return jnp.dot(a.astype(BF16), b.astype(BF16), preferred_element_type=F32)


def _dot_nt(a, b):
    return lax.dot_general(a.astype(BF16), b.astype(BF16), (((1,), (1,)), ((), ())), preferred_element_type=F32)


def _dot_tn(a, b):
    return lax.dot_general(a.astype(BF16), b.astype(BF16), (((0,), (0,)), ((), ())), preferred_element_type=F32)


def _split3(x):
    hi = x.astype(BF16)
    r1 = x - hi.astype(F32)
    mid = r1.astype(BF16)
    lo = (r1 - mid.astype(F32)).astype(BF16)
    return hi, mid, lo


def _dot_f32_lhs(a, b01):
    hi, mid, lo = _split3(a)
    d = lambda x: jnp.dot(x, b01, preferred_element_type=F32)
    return d(hi) + d(mid) + d(lo)


def _dot_f32_rhs(a01, b):
    hi, mid, lo = _split3(b)
    d = lambda x: jnp.dot(a01, x, preferred_element_type=F32)
    return d(hi) + d(mid) + d(lo)


def _dot_x3(a, b):
    ah = a.astype(BF16)
    al = (a - ah.astype(F32)).astype(BF16)
    bh = b.astype(BF16)
    bl = (b - bh.astype(F32)).astype(BF16)
    d = lambda x, y: jnp.dot(x, y, preferred_element_type=F32)
    return d(ah, bh) + d(ah, bl) + d(al, bh)


def _rms(x, gain):
    return x * lax.rsqrt(jnp.mean(x * x, axis=-1, keepdims=True) + NORM_EPS) * gain


def _sigmoid(z):
    return 1.0 / (1.0 + jnp.exp(-z))


def _softplus(z):
    return jnp.maximum(z, 0.0) + jnp.log(1.0 + jnp.exp(-jnp.abs(z)))


def _shift_rows(cur, prev_tail, n):
    rows = lax.broadcasted_iota(jnp.int32, cur.shape, 0)
    out = pltpu.roll(cur, n, axis=0)
    for i in range(n):
        out = jnp.where(rows == i, prev_tail[8 - n + i:8 - n + i + 1, :], out)
    return out


def _inproj_kernel(x_ref, gain_ref, w_ref, o_rwkv, o_conv, o_attn, o_gate):
    h = _rms(x_ref[...], gain_ref[...]).astype(BF16)
    lo = 0
    for o_ref in (o_rwkv, o_conv, o_attn, o_gate):
        width = o_ref.shape[-1]
        for c in range(0, width, 256):
            o_ref[:, c:c + 256] = jnp.dot(h, w_ref[:, lo + c:lo + c + 256],
                                          preferred_element_type=F32).astype(o_ref.dtype)
        lo += width


def _inproj(x2d, gain, w_in_bf16):
    m, d = x2d.shape
    tm = TOKEN_TILE
    widths = (RWKV_COLS, CONV_COLS, ATTN_COLS, w_in_bf16.shape[1] - RWKV_COLS - CONV_COLS - ATTN_COLS)
    dtypes = (F32, BF16, BF16, BF16)
    return pl.pallas_call(
        _inproj_kernel,
        grid=(m // tm,),
        in_specs=[pl.BlockSpec((tm, d), lambda i: (i, 0)),
                  _resident((1, d)),
                  _resident(w_in_bf16.shape)],
        out_specs=[pl.BlockSpec((tm, w), lambda i: (i, 0)) for w in widths],
        out_shape=[jax.ShapeDtypeStruct((m, w), dt) for w, dt in zip(widths, dtypes)],
        compiler_params=_params(("parallel",)),
        name="inproj",
    )(x2d, gain.reshape(1, d), w_in_bf16)


def _rwkv_prep_kernel(p_ref, prev_ref, mu_ref, w0_ref, wup_ref, a0_ref, aup_ref, gup_ref, kk_ref, ka_ref,
                      rk_ref, bd_ref, ltri_ref, lsum_ref, sel_ref,
                      rt_ref, at_ref, bt_ref, kt_ref, bh_ref, kh_ref, v_ref, gam_ref, g_ref, bonus_ref):
    first = pl.program_id(1) == 0
    p = p_ref[0]
    prev = jnp.where(first, 0.0, prev_ref[0])
    xm = p + (_shift_rows(p, prev, 1) - p) * mu_ref[...]
    w = RWKV_WIDTH
    r, k, v = xm[:, 0:w], xm[:, w:2 * w], xm[:, 2 * w:3 * w]
    wa = xm[:, 3 * w:3 * w + LORA_DECAY + LORA_AAA]
    gl = xm[:, 3 * w + LORA_DECAY + LORA_AAA:]
    w_log = -_softplus(-(w0_ref[...] + _dot_x3(jnp.tanh(wa), wup_ref[...]))) - 0.5
    lw = -jnp.exp(w_log)
    a_lr = _sigmoid(a0_ref[...] + _dot_x3(wa, aup_ref[...]))
    g = _dot(_sigmoid(gl), gup_ref[...])
    bd = bd_ref[...]
    kk = k * kk_ref[...]
    kk = kk * lax.rsqrt(jnp.maximum(_dot_f32_lhs(kk * kk, bd), 1e-12))
    k2 = k * (1.0 + (a_lr - 1.0) * ka_ref[...])
    bonus = _dot_f32_lhs(r * k2 * rk_ref[...], bd) * v
    a = -kk
    b = kk * a_lr
    cum = _dot_f32_rhs(ltri_ref[...], lw)
    tot = _dot_f32_rhs(lsum_ref[...], lw)
    e_neg = jnp.exp(-cum)
    e_rem = jnp.exp(tot - cum)
    rt_ref[0] = (r * jnp.exp(cum)).astype(BF16)
    at_ref[0] = (a * jnp.exp(cum - lw)).astype(BF16)
    bt_ref[0] = (b * e_neg).astype(BF16)
    kt_ref[0] = (k2 * e_neg).astype(BF16)
    bh_ref[0] = (b * e_rem).astype(BF16)
    kh_ref[0] = (k2 * e_rem).astype(BF16)
    v_ref[0] = v.astype(BF16)
    gam_ref[0] = jnp.exp(_dot_f32_rhs(sel_ref[...], lw))
    g_ref[0] = g
    bonus_ref[0] = bonus


def _rwkv_prep(p3, mu, w0, w_up, a0, a_up, g_up, k_k, k_a, r_k):
    bsz, seq, cols = p3.shape
    tm = TOKEN_TILE
    w = RWKV_WIDTH
    nck = tm // CHUNK
    lora = LORA_DECAY + LORA_AAA
    wup_pad = jnp.zeros((lora, w), F32).at[:LORA_DECAY].set(w_up)
    aup_pad = jnp.zeros((lora, w), F32).at[LORA_DECAY:].set(a_up)
    hid = jnp.arange(w) // HEAD_DIM
    bd = (hid[:, None] == hid[None, :]).astype(BF16)
    t = jnp.arange(tm)
    same = (t[:, None] // CHUNK) == (t[None, :] // CHUNK)
    ltri = (same & (t[:, None] >= t[None, :])).astype(BF16)
    lsum = same.astype(BF16)
    sel = (jnp.arange(nck)[:, None] == (t[None, :] // CHUNK)).astype(BF16)
    row = lambda vec: vec.reshape(1, -1)
    tok = lambda width: pl.BlockSpec((1, tm, width), lambda b, j: (b, j, 0))
    ins = [tok(cols),
           pl.BlockSpec((1, 8, cols), lambda b, j: (b, jnp.maximum(j * (tm // 8) - 1, 0), 0)),
           _resident((1, cols)), _resident((1, w)), _resident((lora, w)), _resident((1, w)),
           _resident((lora, w)), _resident((LORA_GATE, w)), _resident((1, w)), _resident((1, w)),
           _resident((1, w)), _resident((w, w)), _resident((tm, tm)), _resident((tm, tm)),
           _resident((nck, tm))]
    outs = [tok(w)] * 7 + [pl.BlockSpec((1, nck, w), lambda b, j: (b, j, 0)), tok(w), tok(w)]
    shapes = ([jax.ShapeDtypeStruct((bsz, seq, w), BF16)] * 7
              + [jax.ShapeDtypeStruct((bsz, seq // CHUNK, w), F32)]
              + [jax.ShapeDtypeStruct((bsz, seq, w), F32)] * 2)
    return pl.pallas_call(
        _rwkv_prep_kernel,
        grid=(bsz, seq // tm),
        in_specs=ins, out_specs=outs, out_shape=shapes,
        compiler_params=_params(("parallel", "parallel")),
        name="rwkv_prep",
    )(p3, p3, row(mu), row(w0), wup_pad, row(a0), aup_pad, g_up.astype(BF16), row(k_k), row(k_a),
      row(r_k), bd, ltri, lsum, sel)


def _rows_by_head(z, head_masks):
    z = z.astype(BF16)
    return jnp.concatenate([jnp.where(m, z, jnp.zeros_like(z)) for m in head_masks], axis=0)


def _rwkv_scan_kernel(rt_ref, at_ref, bt_ref, kt_ref, bh_ref, kh_ref, v_ref, gam_ref, g_ref, bonus_ref,
                      lnw_ref, lnb_ref, bd_ref, o_ref, s_scr, y_scr):
    @pl.when(pl.program_id(1) == 0)
    def _():
        s_scr[...] = jnp.zeros_like(s_scr)

    sup = SUPER
    hq = sup // HEAD_DIM
    ncs = sup // CHUNK
    shift = CHUNK.bit_length() - 1
    r2 = lax.broadcasted_iota(jnp.int32, (2 * sup, 2 * sup), 0)
    c2 = lax.broadcasted_iota(jnp.int32, (2 * sup, 2 * sup), 1)
    same_chunk = ((r2 & (sup - 1)) >> shift) == ((c2 & (sup - 1)) >> shift)
    limit = (r2 & (CHUNK - 1)) + jnp.where(r2 >= sup, 1, 0)
    tri = (c2 & (CHUNK - 1)) < jnp.where(same_chunk, limit, 0)
    r1 = lax.broadcasted_iota(jnp.int32, (sup, sup), 0)
    c1 = lax.broadcasted_iota(jnp.int32, (sup, sup), 1)
    eye = (r1 == c1).astype(F32)
    head_diag = (r1 >> shift) == (c1 >> shift)
    head_masks = [(c1 >> shift) == h for h in range(hq)]
    nquad = rt_ref.shape[2] // sup

    def super_body(s, carry):
        rows = pl.ds(pl.multiple_of(s * sup, sup), sup)
        for q in range(nquad):
            lanes = slice(q * sup, (q + 1) * sup)
            rt, at, bt, kt = rt_ref[0, rows, lanes], at_ref[0, rows, lanes], bt_ref[0, rows, lanes], kt_ref[0, rows, lanes]
            bh, kh, v = bh_ref[0, rows, lanes], kh_ref[0, rows, lanes], v_ref[0, rows, lanes]
            keys = jnp.concatenate([bt, kt], axis=0)
            zero = jnp.zeros_like(at)
            grams = []
            for m in head_masks:
                x = jnp.concatenate([jnp.where(m, at, zero), jnp.where(m, rt, zero)], axis=0)
                grams.append(jnp.where(tri, _dot_nt(x, keys), 0.0))
            a_ak = [g[:sup, sup:].astype(BF16) for g in grams]
            a_rb = [g[sup:, :sup].astype(BF16) for g in grams]
            a_rk = [g[sup:, sup:].astype(BF16) for g in grams]
            pw = [g[:sup, :sup] for g in grams]
            t_inv = [eye + p for p in pw]
            for _ in range(5):
                pw = [_dot(p, p) for p in pw]
                t_inv = [t + _dot(t, p) for t, p in zip(t_inv, pw)]
            t_cat = jnp.concatenate([t.astype(BF16) for t in t_inv], axis=1)
            rb_cat = jnp.concatenate(a_rb, axis=1)
            v_h = _rows_by_head(v, head_masks)
            av = _dot(jnp.concatenate(a_ak, axis=1), v_h)
            u = _dot(t_cat, _rows_by_head(av, head_masks))
            wm = _dot(t_cat, _rows_by_head(at, head_masks))
            qh = rt.astype(F32) + _dot(rb_cat, _rows_by_head(wm, head_masks))
            yi = _dot(rb_cat, _rows_by_head(u, head_masks)) + _dot(jnp.concatenate(a_rk, axis=1), v_h)
            mps, nps = [], []
            for c in range(ncs):
                rc = slice(c * CHUNK, (c + 1) * CHUNK)
                mps.append(jnp.where(head_diag, _dot_tn(wm[rc], bh[rc]), 0.0))
                nps.append(jnp.where(head_diag, _dot_tn(jnp.concatenate([u[rc].astype(BF16), v[rc]], axis=0),
                                                        jnp.concatenate([bh[rc], kh[rc]], axis=0)), 0.0))
            st = s_scr[q]
            for c in range(ncs):
                rc = slice(c * CHUNK, (c + 1) * CHUNK)
                y_rows = pl.ds(pl.multiple_of(s * sup + c * CHUNK, CHUNK), CHUNK)
                y_scr[y_rows, lanes] = _dot_nt(qh[rc], st) + yi[rc]
                gam = gam_ref[0, pl.ds(s * ncs + c, 1), lanes]
                st = st * gam + _dot(st, mps[c]) + nps[c]
            s_scr[q] = st
        return carry

    lax.fori_loop(0, rt_ref.shape[1] // sup, super_body, 0)

    y = y_scr[...]
    bd = bd_ref[...]
    inv_n = 1.0 / HEAD_DIM
    mean = _dot_f32_lhs(y, bd) * inv_n
    d = y - mean
    var = _dot_f32_lhs(d * d, bd) * inv_n
    yn = d * lax.rsqrt(var + RWKV_GN_EPS) * lnw_ref[...] + lnb_ref[...]
    o_ref[0] = ((yn + bonus_ref[0]) * g_ref[0]).astype(o_ref.dtype)


def _rwkv_scan(prep, ln_w, ln_b):
    rt = prep[0]
    bsz, seq, w = rt.shape
    tm = TOKEN_TILE
    nck = tm // CHUNK
    hid = jnp.arange(w) // HEAD_DIM
    bd = (hid[:, None] == hid[None, :]).astype(BF16)
    tok = pl.BlockSpec((1, tm, w), lambda b, j: (b, j, 0))
    ins = [tok] * 7 + [pl.BlockSpec((1, nck, w), lambda b, j: (b, j, 0)), tok, tok,
                       _resident((1, w)), _resident((1, w)), _resident((w, w))]
    return pl.pallas_call(
        _rwkv_scan_kernel,
        grid=(bsz, seq // tm),
        in_specs=ins, out_specs=tok,
        out_shape=jax.ShapeDtypeStruct((bsz, seq, w), BF16),
        scratch_shapes=[pltpu.VMEM((w // SUPER, SUPER, SUPER), F32), pltpu.VMEM((tm, w), F32)],
        compiler_params=_params(("parallel", "arbitrary")),
        name="rwkv_scan",
    )(*prep, ln_w.reshape(1, w), ln_b.reshape(1, w), bd)


ATTN_PLANE = 128


def _attn_kernel(q_ref, kp_ref, kc_ref, vp_ref, vc_ref, bias_ref, o_ref, l_ref,
                 q_scr, k_scr, v_scr, o_scr, l_scr, *, dilation):
    blk = ATTN_BLK
    span = blk * dilation
    tile = q_ref.shape[1]
    nplane = q_ref.shape[2] // ATTN_PLANE
    first = pl.program_id(1) == 0
    for j in range(nplane):
        ls = slice(j * ATTN_PLANE, (j + 1) * ATTN_PLANE)
        q_scr[j] = q_ref[0, :, ls].astype(F32)
        k_scr[j, 0:span] = kp_ref[0, :, ls].astype(F32)
        k_scr[j, span:] = kc_ref[0, :, ls].astype(F32)
        v_scr[j, 0:span] = vp_ref[0, :, ls].astype(F32)
        v_scr[j, span:] = vc_ref[0, :, ls].astype(F32)
    col = lax.broadcasted_iota(jnp.int32, (blk, 2 * blk), 1)
    lane = lax.broadcasted_iota(jnp.int32, (blk, ATTN_PLANE), 1)

    def block_body(idx, carry):
        start = (idx % dilation) + (idx // dilation) * span
        if dilation == 1:
            start = pl.multiple_of(start, blk)
            q_rows, k_rows = pl.ds(start, blk), pl.ds(start, 2 * blk)
        else:
            q_rows, k_rows = pl.ds(start, blk, stride=dilation), pl.ds(start, 2 * blk, stride=dilation)
        exists = jnp.logical_or(jnp.logical_or(jnp.logical_not(first), idx >= dilation), col >= blk)
        for j in range(nplane):
            qp = q_scr[j, q_rows, :].astype(BF16)
            kp = k_scr[j, k_rows, :].astype(BF16)
            vp = v_scr[j, k_rows, :].astype(BF16)
            o_pair = None
            l_pair = None
            for hh in range(ATTN_PLANE // HEAD_DIM):
                mine = (lane // HEAD_DIM) == hh
                s = _dot_nt(jnp.where(mine, qp, jnp.zeros_like(qp)), kp) * (HEAD_DIM ** -0.5)
                s = jnp.where(exists, s + bias_ref[j * (ATTN_PLANE // HEAD_DIM) + hh], MASK_VALUE)
                m = jnp.max(s, axis=-1, keepdims=True)
                pexp = jnp.exp(s - m)
                l = jnp.sum(pexp, axis=-1, keepdims=True)
                o = _dot(pexp, vp) / l
                lse = jnp.broadcast_to(m + jnp.log(l), o.shape)
                o_pair = o if o_pair is None else jnp.where(mine, o, o_pair)
                l_pair = lse if l_pair is None else jnp.where(mine, lse, l_pair)
            o_scr[j, q_rows, :] = o_pair
            l_scr[j, q_rows, :] = l_pair
        return carry

    lax.fori_loop(0, tile // blk, block_body, 0)
    for j in range(nplane):
        ls = slice(j * ATTN_PLANE, (j + 1) * ATTN_PLANE)
        o_ref[0, :, ls] = o_scr[j]
        l_ref[0, :, ls] = l_scr[j]


def _attn_bias(group, dilation):
    blk = ATTN_BLK
    heads = jnp.arange(group * HEADS_PER_GROUP + 1, (group + 1) * HEADS_PER_GROUP + 1, dtype=F32)
    slopes = jnp.exp2(-ALIBI_MAX_EXP * heads / ATTN_HEADS)
    steps = blk + jnp.arange(blk)[:, None] - jnp.arange(2 * blk)[None, :]
    in_window = (steps >= 0) & (steps <= blk)
    alibi = -slopes[:, None, None] * (steps * dilation).astype(F32)[None]
    return jnp.where(in_window[None], alibi, MASK_VALUE)


def _attn_group(p_attn3, group, dilation):
    bsz, seq, cols = p_attn3.shape
    blk = ATTN_BLK
    gw = ATTN_OUT_WIDTH
    span = blk * dilation
    tile = max(span, TOKEN_TILE)
    nplane = gw // ATTN_PLANE
    per_kind = ATTN_WIDTH // gw

    def cur(kind):
        return pl.BlockSpec((1, tile, gw), lambda b, n: (b, n, kind * per_kind + group))

    def prev(kind):
        return pl.BlockSpec((1, span, gw),
                            lambda b, n: (b, jnp.maximum(n * (tile // span) - 1, 0), kind * per_kind + group))

    out_spec = pl.BlockSpec((1, tile, gw), lambda b, n: (b, n, 0))
    plane = lambda rows: pltpu.VMEM((nplane, rows, ATTN_PLANE), F32)
    return pl.pallas_call(
        functools.partial(_attn_kernel, dilation=dilation),
        grid=(bsz, seq // tile),
        in_specs=[cur(0), prev(1), cur(1), prev(2), cur(2), _resident((HEADS_PER_GROUP, blk, 2 * blk))],
        out_specs=[out_spec, out_spec],
        out_shape=[jax.ShapeDtypeStruct((bsz, seq, gw), F32)] * 2,
        scratch_shapes=[plane(tile), plane(span + tile), plane(span + tile), plane(tile), plane(tile)],
        compiler_params=_params(("parallel", "arbitrary")),
        name=f"attn_g{group}",
    )(p_attn3, p_attn3, p_attn3, p_attn3, p_attn3, _attn_bias(group, dilation))


def _merge_kernel(x_ref, ya_ref, pc_ref, pcprev_ref, o0_ref, o1_ref, o2_ref, l0_ref, l1_ref, l2_ref,
                  gate_ref, convw_ref, wb_ref, wout_ref, gain_ref, out_ref):
    first = pl.program_id(1) == 0
    cw = CONV_WIDTH
    pc = pc_ref[0].astype(F32)
    pprev = pcprev_ref[0].astype(F32)
    u = pc[:, cw:2 * cw] * pc[:, 2 * cw:]
    uprev = jnp.where(first, 0.0, pprev[:, cw:2 * cw] * pprev[:, 2 * cw:])
    cwt = convw_ref[...]
    yb = pc[:, :cw] * (cwt[0:1] * _shift_rows(u, uprev, 2) + cwt[1:2] * _shift_rows(u, uprev, 1) + cwt[2:3] * u)

    l0, l1, l2 = l0_ref[0], l1_ref[0], l2_ref[0]
    mx = jnp.maximum(jnp.maximum(l0, l1), l2)
    e0, e1, e2 = jnp.exp(l0 - mx), jnp.exp(l1 - mx), jnp.exp(l2 - mx)
    yc = (e0 * o0_ref[0] + e1 * o1_ref[0] + e2 * o2_ref[0]) / (e0 + e1 + e2)

    d = x_ref.shape[-1]
    gate = gate_ref[0].astype(F32)
    ra, rb = RWKV_WIDTH, RWKV_WIDTH + CONV_WIDTH
    merged = (_sigmoid(gate[:, :d]) * _dot(ya_ref[0], wb_ref[:ra])
              + _sigmoid(gate[:, d:2 * d]) * _dot(yb, wb_ref[ra:rb])
              + _sigmoid(gate[:, 2 * d:]) * _dot(yc, wb_ref[rb:]))
    out_ref[0] = x_ref[0] + _rms(_dot(merged, wout_ref[...]), gain_ref[...])


def _merge(x3, ya, p_conv3, attn, p_gate3, conv_w, wb_bf16, wout_bf16, gain):
    bsz, seq, d = x3.shape
    tm = TOKEN_TILE
    tok = lambda width: pl.BlockSpec((1, tm, width), lambda b, j: (b, j, 0))
    (o0, l0), (o1, l1), (o2, l2) = attn
    gw = ATTN_OUT_WIDTH
    ins = [tok(d), tok(RWKV_WIDTH), tok(CONV_COLS),
           pl.BlockSpec((1, 8, CONV_COLS), lambda b, j: (b, jnp.maximum(j * (tm // 8) - 1, 0), 0)),
           tok(gw), tok(gw), tok(gw), tok(gw), tok(gw), tok(gw), tok(p_gate3.shape[-1]),
           _resident(conv_w.shape), _resident(wb_bf16.shape), _resident(wout_bf16.shape), _resident((1, d))]
    return pl.pallas_call(
        _merge_kernel,
        grid=(bsz, seq // tm),
        in_specs=ins, out_specs=tok(d),
        out_shape=jax.ShapeDtypeStruct((bsz, seq, d), F32),
        compiler_params=_params(("parallel", "parallel")),
        name="merge",
    )(x3, ya, p_conv3, p_conv3, o0, o1, o2, l0, l1, l2, p_gate3, conv_w, wb_bf16, wout_bf16, gain.reshape(1, d))


def _ffn_kernel(x_ref, gpre_ref, gpost_ref, win_ref, wout_ref, out_ref, act_scr):
    x = x_ref[...]
    h = _rms(x, gpre_ref[...]).astype(BF16)
    dff = wout_ref.shape[0]
    for c in range(0, dff, 256):
        gate = jnp.dot(h, win_ref[:, c:c + 256], preferred_element_type=F32)
        up = jnp.dot(h, win_ref[:, dff + c:dff + c + 256], preferred_element_type=F32)
        act_scr[:, c:c + 256] = (gate * _sigmoid(gate) * up).astype(BF16)
    z = jnp.dot(act_scr[...], wout_ref[...], preferred_element_type=F32)
    out_ref[...] = x + _rms(z, gpost_ref[...])


def _ffn(x2d, gpre, gpost, win_bf16, wout_bf16):
    m, d = x2d.shape
    tm = TOKEN_TILE
    dff = wout_bf16.shape[0]
    return pl.pallas_call(
        _ffn_kernel,
        grid=(m // tm,),
        in_specs=[pl.BlockSpec((tm, d), lambda i: (i, 0)), _resident((1, d)), _resident((1, d)),
                  _resident(win_bf16.shape), _resident(wout_bf16.shape)],
        out_specs=pl.BlockSpec((tm, d), lambda i: (i, 0)),
        out_shape=jax.ShapeDtypeStruct((m, d), F32),
        scratch_shapes=[pltpu.VMEM((tm, dff), BF16)],
        compiler_params=_params(("parallel",)),
        name="ffn",
    )(x2d, gpre.reshape(1, d), gpost.reshape(1, d), win_bf16, wout_bf16)


def _layer(x3, norm_mix_pre, norm_mix_post, norm_ffn_pre, norm_ffn_post, w_in, rwkv_mu, rwkv_w0, rwkv_w_up,
           rwkv_a0, rwkv_a_up, rwkv_g_up, rwkv_k_k, rwkv_k_a, rwkv_r_k, rwkv_ln_w, rwkv_ln_b, conv_w,
           w_branch, w_out, w_ffn_in, w_ffn_out):
    bsz, seq, d = x3.shape
    m = bsz * seq
    p_rwkv, p_conv, p_attn, p_gate = _inproj(x3.reshape(m, d), norm_mix_pre, w_in.astype(BF16))
    in3 = lambda t: t.reshape(bsz, seq, t.shape[-1])
    prep = _rwkv_prep(in3(p_rwkv), rwkv_mu, rwkv_w0, rwkv_w_up, rwkv_a0, rwkv_a_up, rwkv_g_up,
                      rwkv_k_k, rwkv_k_a, rwkv_r_k.reshape(-1))
    ya = _rwkv_scan(prep, rwkv_ln_w, rwkv_ln_b)
    attn = [_attn_group(in3(p_attn), gi, dil) for gi, (_, dil) in enumerate(ATTN_GROUPS)]
    x3 = _merge(x3, ya, in3(p_conv), attn, in3(p_gate), conv_w, w_branch.astype(BF16), w_out.astype(BF16),
                norm_mix_post)
    x2 = _ffn(x3.reshape(m, d), norm_ffn_pre, norm_ffn_post, w_ffn_in.astype(BF16), w_ffn_out.astype(BF16))
    return x2.reshape(bsz, seq, d)


def kernel(x, norm_mix_pre, norm_mix_post, norm_ffn_pre, norm_ffn_post, w_in, rwkv_mu, rwkv_w0, rwkv_w_up, rwkv_a0, rwkv_a_up, rwkv_g_up, rwkv_k_k, rwkv_k_a, rwkv_r_k, rwkv_ln_w, rwkv_ln_b, conv_w, w_branch, w_out, w_ffn_in, w_ffn_out):
    params = (norm_mix_pre, norm_mix_post, norm_ffn_pre, norm_ffn_post, w_in, rwkv_mu, rwkv_w0, rwkv_w_up,
              rwkv_a0, rwkv_a_up, rwkv_g_up, rwkv_k_k, rwkv_k_a, rwkv_r_k, rwkv_ln_w, rwkv_ln_b, conv_w,
              w_branch, w_out, w_ffn_in, w_ffn_out)
    for layer in range(w_in.shape[0]):
        x = _layer(x, *(p[layer] for p in params))
    return x
```

```python
import functools

import jax
import jax.numpy as jnp
from jax import lax
from jax.experimental import pallas as pl
from jax.experimental.pallas import tpu as pltpu

F32 = jnp.float32
BF16 = jnp.bfloat16

HEAD_DIM = 64
NORM_EPS = 1e-6
RWKV_HEADS = 8
RWKV_WIDTH = RWKV_HEADS * HEAD_DIM
LORA_DECAY = 64
LORA_AAA = 64
LORA_GATE = 128
RWKV_COLS = 3 * RWKV_WIDTH + LORA_DECAY + LORA_AAA + LORA_GATE
RWKV_GN_EPS = 64e-5
CONV_WIDTH = 512
CONV_COLS = 3 * CONV_WIDTH
ATTN_GROUPS = ((128, 1), (512, 4), (2048, 16))
HEADS_PER_GROUP = 4
ATTN_HEADS = HEADS_PER_GROUP * len(ATTN_GROUPS)
ATTN_WIDTH = ATTN_HEADS * HEAD_DIM
ATTN_COLS = 3 * ATTN_WIDTH
ATTN_OUT_WIDTH = HEADS_PER_GROUP * HEAD_DIM
ALIBI_MAX_EXP = 8.0
ATTN_BLK = 128
MASK_VALUE = -1e30

MXU_TILE = 256
CHUNK = 64
QUAD_LANES = MXU_TILE
SCAN_GROUP = 4
TOKEN_TILE = 512
VMEM_LIMIT = 56 * 1024 * 1024


def _params(sem):
    return pltpu.CompilerParams(dimension_semantics=sem, vmem_limit_bytes=VMEM_LIMIT)


def _resident(shape):
    zeros = (0,) * len(shape)
    return pl.BlockSpec(shape, lambda *_: zeros, pipeline_mode=pl.Buffered(1))


def _layer_resident(stacked_shape, layer):
    tail = tuple(stacked_shape[1:])
    zeros = (0,) * len(tail)
    return pl.BlockSpec((None,) + tail, lambda *_: (layer,) + zeros, pipeline_mode=pl.Buffered(1))


def _dot(a, b):
    return jnp.dot(a.astype(BF16), b.astype(BF16), preferred_element_type=F32)


def _dot_nt(a, b):
    return lax.dot_general(a.astype(BF16), b.astype(BF16), (((1,), (1,)), ((), ())), preferred_element_type=F32)


def _dot_tn(a, b):
    return lax.dot_general(a.astype(BF16), b.astype(BF16), (((0,), (0,)), ((), ())), preferred_element_type=F32)


def _split3(x):
    hi = x.astype(BF16)
    r1 = x - hi.astype(F32)
    mid = r1.astype(BF16)
    lo = (r1 - mid.astype(F32)).astype(BF16)
    return hi, mid, lo


def _dot_f32_lhs(a, b01, terms=3):
    return sum(jnp.dot(x, b01, preferred_element_type=F32) for x in _split3(a)[:terms])


def _dot_x3(a, b):
    ah = a.astype(BF16)
    al = (a - ah.astype(F32)).astype(BF16)
    bh = b.astype(BF16)
    bl = (b - bh.astype(F32)).astype(BF16)
    d = lambda x, y: jnp.dot(x, y, preferred_element_type=F32)
    return d(ah, bh) + d(ah, bl) + d(al, bh)


def _rms(x, gain):
    return x * lax.rsqrt(jnp.mean(x * x, axis=-1, keepdims=True) + NORM_EPS) * gain


def _sigmoid(z):
    return 1.0 / (1.0 + jnp.exp(-z))


def _softplus(z):
    return jnp.maximum(z, 0.0) + jnp.log(1.0 + jnp.exp(-jnp.abs(z)))


def _shift_rows(cur, prev_tail, n):
    rows = lax.broadcasted_iota(jnp.int32, cur.shape, 0)
    out = pltpu.roll(cur, n, axis=0)
    for i in range(n):
        out = jnp.where(rows == i, prev_tail[8 - n + i:8 - n + i + 1, :], out)
    return out


def _inproj_kernel(x_ref, gain_ref, w_ref, o_rwkv, o_conv, o_attn, o_gate):
    h = _rms(x_ref[...], gain_ref[...]).astype(BF16)
    lo = 0
    for o_ref in (o_rwkv, o_conv, o_attn, o_gate):
        width = o_ref.shape[-1]
        for c in range(0, width, MXU_TILE):
            o_ref[:, c:c + MXU_TILE] = jnp.dot(h, w_ref[:, lo + c:lo + c + MXU_TILE],
                                               preferred_element_type=F32).astype(o_ref.dtype)
        lo += width


def _inproj(x2d, gain, w_in_bf16, layer):
    m, d = x2d.shape
    tm = TOKEN_TILE
    widths = (RWKV_COLS, CONV_COLS, ATTN_COLS, w_in_bf16.shape[2] - RWKV_COLS - CONV_COLS - ATTN_COLS)
    dtypes = (F32, BF16, BF16, BF16)
    return pl.pallas_call(
        _inproj_kernel,
        grid=(m // tm,),
        in_specs=[pl.BlockSpec((tm, d), lambda i: (i, 0)),
                  _resident((1, d)),
                  _layer_resident(w_in_bf16.shape, layer)],
        out_specs=[pl.BlockSpec((tm, w), lambda i: (i, 0)) for w in widths],
        out_shape=[jax.ShapeDtypeStruct((m, w), dt) for w, dt in zip(widths, dtypes)],
        compiler_params=_params(("parallel",)),
        name="inproj",
    )(x2d, gain.reshape(1, d), w_in_bf16)


def _rwkv_prep_kernel(p_ref, prev_ref, mu_ref, w0_ref, wup_ref, a0_ref, aup_ref, gup_ref, kk_ref, ka_ref,
                      rk_ref, bd_ref, ltri_ref,
                      rt_ref, at_ref, bt_ref, kt_ref, bh_ref, kh_ref, v_ref, gam_ref, g_ref, bonus_ref):
    first = pl.program_id(1) == 0
    p = p_ref[0]
    prev = jnp.where(first, 0.0, prev_ref[0])
    xm = p + (_shift_rows(p, prev, 1) - p) * mu_ref[...]
    w = RWKV_WIDTH
    r, k, v = xm[:, 0:w], xm[:, w:2 * w], xm[:, 2 * w:3 * w]
    wa = xm[:, 3 * w:3 * w + LORA_DECAY + LORA_AAA]
    gl = xm[:, 3 * w + LORA_DECAY + LORA_AAA:]
    w_log = -_softplus(-(w0_ref[...] + _dot_x3(jnp.tanh(wa), wup_ref[...]))) - 0.5
    lw = -jnp.exp(w_log)
    a_lr = _sigmoid(a0_ref[...] + _dot(wa, aup_ref[...]))
    g = _dot(_sigmoid(gl), gup_ref[...])
    bd = bd_ref[...]
    kk = k * kk_ref[...]
    kk = kk * lax.rsqrt(jnp.maximum(_dot_f32_lhs(kk * kk, bd, terms=2), 1e-12))
    k2 = k * (1.0 + (a_lr - 1.0) * ka_ref[...])
    bonus = _dot(r * k2 * rk_ref[...], bd) * v
    a = -kk
    b = kk * a_lr
    ltri = ltri_ref[...]
    parts = _split3(lw)
    cums, tots = [], []
    for c in range(p.shape[0] // CHUNK):
        rc = slice(c * CHUNK, (c + 1) * CHUNK)
        cum_c = sum(jnp.dot(ltri, x[rc], preferred_element_type=F32) for x in parts)
        cums.append(cum_c)
        tots.append(cum_c[CHUNK - 1:CHUNK])
    cum = jnp.concatenate(cums, axis=0)
    tot = jnp.concatenate([jnp.broadcast_to(t, (CHUNK, t.shape[1])) for t in tots], axis=0)
    e_neg = jnp.exp(-cum)
    e_rem = jnp.exp(tot - cum)
    rt_ref[0] = (r * jnp.exp(cum)).astype(BF16)
    at_ref[0] = (a * jnp.exp(cum - lw)).astype(BF16)
    bt_ref[0] = (b * e_neg).astype(BF16)
    kt_ref[0] = (k2 * e_neg).astype(BF16)
    bh_ref[0] = (b * e_rem).astype(BF16)
    kh_ref[0] = (k2 * e_rem).astype(BF16)
    v_ref[0] = v.astype(BF16)
    gam_ref[0] = jnp.exp(jnp.concatenate(tots, axis=0))
    g_ref[0] = g
    bonus_ref[0] = bonus


def _rwkv_prep(p3, mu, w0, w_up, a0, a_up, g_up, k_k, k_a, r_k):
    bsz, seq, cols = p3.shape
    tm = TOKEN_TILE
    w = RWKV_WIDTH
    nck = tm // CHUNK
    lora = LORA_DECAY + LORA_AAA
    wup_pad = jnp.zeros((lora, w), F32).at[:LORA_DECAY].set(w_up)
    aup_pad = jnp.zeros((lora, w), F32).at[LORA_DECAY:].set(a_up)
    hid = jnp.arange(w) // HEAD_DIM
    bd = (hid[:, None] == hid[None, :]).astype(BF16)
    t = jnp.arange(CHUNK)
    ltri = (t[:, None] >= t[None, :]).astype(BF16)
    row = lambda vec: vec.reshape(1, -1)
    tok = lambda width: pl.BlockSpec((1, tm, width), lambda b, j: (b, j, 0))
    ins = [tok(cols),
           pl.BlockSpec((1, 8, cols), lambda b, j: (b, jnp.maximum(j * (tm // 8) - 1, 0), 0)),
           _resident((1, cols)), _resident((1, w)), _resident((lora, w)), _resident((1, w)),
           _resident((lora, w)), _resident((LORA_GATE, w)), _resident((1, w)), _resident((1, w)),
           _resident((1, w)), _resident((w, w)), _resident((CHUNK, CHUNK))]
    outs = [tok(w)] * 7 + [pl.BlockSpec((1, nck, w), lambda b, j: (b, j, 0)), tok(w), tok(w)]
    shapes = ([jax.ShapeDtypeStruct((bsz, seq, w), BF16)] * 7
              + [jax.ShapeDtypeStruct((bsz, seq // CHUNK, w), F32)]
              + [jax.ShapeDtypeStruct((bsz, seq, w), F32)] * 2)
    return pl.pallas_call(
        _rwkv_prep_kernel,
        grid=(bsz, seq // tm),
        in_specs=ins, out_specs=outs, out_shape=shapes,
        compiler_params=_params(("parallel", "parallel")),
        name="rwkv_prep",
    )(p3, p3, row(mu), row(w0), wup_pad, row(a0), aup_pad.astype(BF16), g_up.astype(BF16), row(k_k), row(k_a),
      row(r_k), bd, ltri)


def _head_tile(z, head_diag):
    z = z.astype(BF16)
    reps = head_diag.shape[0] // z.shape[0]
    return jnp.where(head_diag, jnp.concatenate([z] * reps, axis=0), jnp.zeros(head_diag.shape, BF16))


def _rwkv_scan_kernel(rt_ref, at_ref, bt_ref, kt_ref, bh_ref, kh_ref, v_ref, gam_ref, g_ref, bonus_ref,
                      lnw_ref, lnb_ref, bd_ref, o_ref, s_scr, y_scr):
    @pl.when(pl.program_id(1) == 0)
    def _():
        s_scr[...] = jnp.zeros_like(s_scr)

    quad = QUAD_LANES
    r1 = lax.broadcasted_iota(jnp.int32, (quad, quad), 0)
    c1 = lax.broadcasted_iota(jnp.int32, (quad, quad), 1)
    head_diag = (r1 // CHUNK) == (c1 // HEAD_DIM)
    ts = lax.broadcasted_iota(jnp.int32, (CHUNK, quad), 0)
    ss = lax.broadcasted_iota(jnp.int32, (CHUNK, quad), 1) % HEAD_DIM
    strict = ss < ts
    incl = ss <= ts
    eye = (ss == ts).astype(F32)
    nquad = rt_ref.shape[2] // quad
    units = [(c, q) for c in range(SCAN_GROUP) for q in range(nquad)]
    tile = functools.partial(_head_tile, head_diag=head_diag)
    lane_cat = lambda x, y: jnp.concatenate([x, y], axis=1)
    row_cat = lambda x, y: jnp.concatenate([x, y], axis=0)

    def group_body(gi, carry):
        base = gi * (SCAN_GROUP * CHUNK)
        ops = []
        for c, q in units:
            rows = pl.ds(pl.multiple_of(base + c * CHUNK, CHUNK), CHUNK)
            lanes = slice(q * quad, (q + 1) * quad)
            ops.append([ref[0, rows, lanes] for ref in (rt_ref, at_ref, bt_ref, kt_ref, bh_ref, kh_ref, v_ref)])
        rt, at, bt, kt, bh, kh, v = (list(x) for x in zip(*ops))
        rng = range(len(units))
        gram = [_dot_nt(row_cat(at[i], rt[i]), row_cat(tile(bt[i]), tile(kt[i]))) for i in rng]
        pw = [jnp.where(strict, gram[i][:CHUNK, :quad], 0.0) for i in rng]
        akrk = [row_cat(jnp.where(strict, gram[i][:CHUNK, quad:], 0.0),
                        jnp.where(incl, gram[i][CHUNK:, quad:], 0.0)).astype(BF16) for i in rng]
        a_rb = [jnp.where(incl, gram[i][CHUNK:, :quad], 0.0).astype(BF16) for i in rng]
        t_inv = [eye + pw[i] for i in rng]
        for _ in range(5):
            pw = [_dot(pw[i], tile(pw[i])) for i in rng]
            t_inv = [t_inv[i] + _dot(t_inv[i], tile(pw[i])) for i in rng]
        avrk = [_dot(akrk[i], tile(v[i])) for i in rng]
        uw = [_dot(t_inv[i], lane_cat(tile(avrk[i][:CHUNK]), tile(at[i]))) for i in rng]
        u = [x[:, :quad] for x in uw]
        wm = [x[:, quad:] for x in uw]
        qy = [_dot(a_rb[i], lane_cat(tile(wm[i]), tile(u[i]))) for i in rng]
        qh = [rt[i].astype(F32) + qy[i][:, :quad] for i in rng]
        yi = [qy[i][:, quad:] + avrk[i][CHUNK:] for i in rng]
        mp = [jnp.where(head_diag, _dot_tn(wm[i], bh[i]), 0.0) for i in rng]
        nps = [jnp.where(head_diag, _dot_tn(row_cat(u[i].astype(BF16), v[i]), row_cat(bh[i], kh[i])), 0.0)
               for i in rng]
        st = [s_scr[q] for q in range(nquad)]
        for i, (c, q) in enumerate(units):
            lanes = slice(q * quad, (q + 1) * quad)
            y_scr[pl.ds(pl.multiple_of(base + c * CHUNK, CHUNK), CHUNK), lanes] = _dot_nt(qh[i], st[q]) + yi[i]
            gam = gam_ref[0, pl.ds(gi * SCAN_GROUP + c, 1), lanes]
            st[q] = st[q] * gam + _dot(st[q], mp[i]) + nps[i]
        for q in range(nquad):
            s_scr[q] = st[q]
        return carry

    lax.fori_loop(0, rt_ref.shape[1] // (SCAN_GROUP * CHUNK), group_body, 0)

    y = y_scr[...]
    bd = bd_ref[...]
    inv_n = 1.0 / HEAD_DIM
    mean = _dot_f32_lhs(y, bd, terms=2) * inv_n
    d = y - mean
    var = _dot_f32_lhs(d * d, bd, terms=2) * inv_n
    yn = d * lax.rsqrt(var + RWKV_GN_EPS) * lnw_ref[...] + lnb_ref[...]
    o_ref[0] = ((yn + bonus_ref[0]) * g_ref[0]).astype(o_ref.dtype)


def _rwkv_scan(prep, ln_w, ln_b):
    rt = prep[0]
    bsz, seq, w = rt.shape
    tm = TOKEN_TILE
    nck = tm // CHUNK
    hid = jnp.arange(w) // HEAD_DIM
    bd = (hid[:, None] == hid[None, :]).astype(BF16)
    tok = pl.BlockSpec((1, tm, w), lambda b, j: (b, j, 0))
    ins = [tok] * 7 + [pl.BlockSpec((1, nck, w), lambda b, j: (b, j, 0)), tok, tok,
                       _resident((1, w)), _resident((1, w)), _resident((w, w))]
    return pl.pallas_call(
        _rwkv_scan_kernel,
        grid=(bsz, seq // tm),
        in_specs=ins, out_specs=tok,
        out_shape=jax.ShapeDtypeStruct((bsz, seq, w), BF16),
        scratch_shapes=[pltpu.VMEM((w // QUAD_LANES, QUAD_LANES, QUAD_LANES), F32), pltpu.VMEM((tm, w), F32)],
        compiler_params=_params(("parallel", "arbitrary")),
        name="rwkv_scan",
    )(*prep, ln_w.reshape(1, w), ln_b.reshape(1, w), bd)


ATTN_PLANE = 128


def _attn_kernel(q_ref, kp_ref, kc_ref, vp_ref, vc_ref, bias_ref, o_ref, l_ref,
                 q_scr, k_scr, v_scr, o_scr, l_scr, *, dilation):
    blk = ATTN_BLK
    span = blk * dilation
    tile = q_ref.shape[1]
    nplane = q_ref.shape[2] // ATTN_PLANE
    first = pl.program_id(1) == 0
    for j in range(nplane):
        ls = slice(j * ATTN_PLANE, (j + 1) * ATTN_PLANE)
        q_scr[j] = q_ref[0, :, ls].astype(F32) * (HEAD_DIM ** -0.5)
        k_scr[j, 0:span] = kp_ref[0, :, ls].astype(F32)
        k_scr[j, span:] = kc_ref[0, :, ls].astype(F32)
        v_scr[j, 0:span] = vp_ref[0, :, ls].astype(F32)
        v_scr[j, span:] = vc_ref[0, :, ls].astype(F32)
    col = lax.broadcasted_iota(jnp.int32, (blk, 2 * blk), 1)
    lane = lax.broadcasted_iota(jnp.int32, (blk, ATTN_PLANE), 1)
    per_plane = ATTN_PLANE // HEAD_DIM
    mine = [(lane // HEAD_DIM) == hh for hh in range(per_plane)]
    heads = [(j, hh) for j in range(nplane) for hh in range(per_plane)]

    def block_body(idx, carry):
        start = (idx % dilation) + (idx // dilation) * span
        if dilation == 1:
            start = pl.multiple_of(start, blk)
            q_rows, k_rows = pl.ds(start, blk), pl.ds(start, 2 * blk)
        else:
            q_rows, k_rows = pl.ds(start, blk, stride=dilation), pl.ds(start, 2 * blk, stride=dilation)
        exists = jnp.logical_or(jnp.logical_or(jnp.logical_not(first), idx >= dilation), col >= blk)
        qp = [q_scr[j, q_rows, :].astype(BF16) for j in range(nplane)]
        kp = [k_scr[j, k_rows, :].astype(BF16) for j in range(nplane)]
        vp = [v_scr[j, k_rows, :].astype(BF16) for j in range(nplane)]
        s = [_dot_nt(jnp.where(mine[hh], qp[j], jnp.zeros_like(qp[j])), kp[j]) for j, hh in heads]
        s = [jnp.where(exists, s[i] + bias_ref[i], MASK_VALUE) for i in range(len(heads))]
        m = [jnp.max(x, axis=-1, keepdims=True) for x in s]
        pexp = [jnp.exp(x - mx) for x, mx in zip(s, m)]
        l = [jnp.sum(x, axis=-1, keepdims=True) for x in pexp]
        o = [_dot(pexp[i], vp[j]) / l[i] for i, (j, hh) in enumerate(heads)]
        lse = [mx + jnp.log(lx) for mx, lx in zip(m, l)]
        for j in range(nplane):
            o_pair = o[j * per_plane]
            l_pair = jnp.broadcast_to(lse[j * per_plane], o_pair.shape)
            for hh in range(1, per_plane):
                o_pair = jnp.where(mine[hh], o[j * per_plane + hh], o_pair)
                l_pair = jnp.where(mine[hh], lse[j * per_plane + hh], l_pair)
            o_scr[j, q_rows, :] = o_pair
            l_scr[j, q_rows, :] = l_pair
        return carry

    lax.fori_loop(0, tile // blk, block_body, 0)
    for j in range(nplane):
        ls = slice(j * ATTN_PLANE, (j + 1) * ATTN_PLANE)
        o_ref[0, :, ls] = o_scr[j]
        l_ref[0, :, ls] = l_scr[j]


def _attn_bias(group, dilation):
    blk = ATTN_BLK
    heads = jnp.arange(group * HEADS_PER_GROUP + 1, (group + 1) * HEADS_PER_GROUP + 1, dtype=F32)
    slopes = jnp.exp2(-ALIBI_MAX_EXP * heads / ATTN_HEADS)
    steps = blk + jnp.arange(blk)[:, None] - jnp.arange(2 * blk)[None, :]
    in_window = (steps >= 0) & (steps <= blk)
    alibi = -slopes[:, None, None] * (steps * dilation).astype(F32)[None]
    return jnp.where(in_window[None], alibi, MASK_VALUE)


def _attn_group(p_attn3, group, dilation):
    bsz, seq, cols = p_attn3.shape
    blk = ATTN_BLK
    gw = ATTN_OUT_WIDTH
    span = blk * dilation
    tile = max(span, TOKEN_TILE)
    nplane = gw // ATTN_PLANE
    per_kind = ATTN_WIDTH // gw

    def cur(kind):
        return pl.BlockSpec((1, tile, gw), lambda b, n: (b, n, kind * per_kind + group))

    def prev(kind):
        return pl.BlockSpec((1, span, gw),
                            lambda b, n: (b, jnp.maximum(n * (tile // span) - 1, 0), kind * per_kind + group))

    out_spec = pl.BlockSpec((1, tile, gw), lambda b, n: (b, n, 0))
    plane = lambda rows: pltpu.VMEM((nplane, rows, ATTN_PLANE), F32)
    return pl.pallas_call(
        functools.partial(_attn_kernel, dilation=dilation),
        grid=(bsz, seq // tile),
        in_specs=[cur(0), prev(1), cur(1), prev(2), cur(2), _resident((HEADS_PER_GROUP, blk, 2 * blk))],
        out_specs=[out_spec, out_spec],
        out_shape=[jax.ShapeDtypeStruct((bsz, seq, gw), F32)] * 2,
        scratch_shapes=[plane(tile), plane(span + tile), plane(span + tile), plane(tile), plane(tile)],
        compiler_params=_params(("parallel", "arbitrary")),
        name=f"attn_g{group}",
    )(p_attn3, p_attn3, p_attn3, p_attn3, p_attn3, _attn_bias(group, dilation))


def _merge_kernel(x_ref, ya_ref, pc_ref, pcprev_ref, o0_ref, o1_ref, o2_ref, l0_ref, l1_ref, l2_ref,
                  gate_ref, convw_ref, wb_ref, wout_ref, gain_ref, out_ref):
    first = pl.program_id(1) == 0
    cw = CONV_WIDTH
    pc = pc_ref[0].astype(F32)
    pprev = pcprev_ref[0].astype(F32)
    u = pc[:, cw:2 * cw] * pc[:, 2 * cw:]
    uprev = jnp.where(first, 0.0, pprev[:, cw:2 * cw] * pprev[:, 2 * cw:])
    cwt = convw_ref[...]
    yb = pc[:, :cw] * (cwt[0:1] * _shift_rows(u, uprev, 2) + cwt[1:2] * _shift_rows(u, uprev, 1) + cwt[2:3] * u)

    l0, l1, l2 = l0_ref[0], l1_ref[0], l2_ref[0]
    mx = jnp.maximum(jnp.maximum(l0, l1), l2)
    e0, e1, e2 = jnp.exp(l0 - mx), jnp.exp(l1 - mx), jnp.exp(l2 - mx)
    yc = (e0 * o0_ref[0] + e1 * o1_ref[0] + e2 * o2_ref[0]) / (e0 + e1 + e2)

    d = x_ref.shape[-1]
    gate = gate_ref[0].astype(F32)
    ra, rb = RWKV_WIDTH, RWKV_WIDTH + CONV_WIDTH
    merged = (_sigmoid(gate[:, :d]) * _dot(ya_ref[0], wb_ref[:ra])
              + _sigmoid(gate[:, d:2 * d]) * _dot(yb, wb_ref[ra:rb])
              + _sigmoid(gate[:, 2 * d:]) * _dot(yc, wb_ref[rb:]))
    out_ref[0] = x_ref[0] + _rms(_dot(merged, wout_ref[...]), gain_ref[...])


def _merge(x3, ya, p_conv3, attn, p_gate3, conv_w, wb_bf16, wout_bf16, gain, layer):
    bsz, seq, d = x3.shape
    tm = TOKEN_TILE
    tok = lambda width: pl.BlockSpec((1, tm, width), lambda b, j: (b, j, 0))
    (o0, l0), (o1, l1), (o2, l2) = attn
    gw = ATTN_OUT_WIDTH
    ins = [tok(d), tok(RWKV_WIDTH), tok(CONV_COLS),
           pl.BlockSpec((1, 8, CONV_COLS), lambda b, j: (b, jnp.maximum(j * (tm // 8) - 1, 0), 0)),
           tok(gw), tok(gw), tok(gw), tok(gw), tok(gw), tok(gw), tok(p_gate3.shape[-1]),
           _resident(conv_w.shape), _layer_resident(wb_bf16.shape, layer), _layer_resident(wout_bf16.shape, layer),
           _resident((1, d))]
    return pl.pallas_call(
        _merge_kernel,
        grid=(bsz, seq // tm),
        in_specs=ins, out_specs=tok(d),
        out_shape=jax.ShapeDtypeStruct((bsz, seq, d), F32),
        compiler_params=_params(("parallel", "parallel")),
        name="merge",
    )(x3, ya, p_conv3, p_conv3, o0, o1, o2, l0, l1, l2, p_gate3, conv_w, wb_bf16, wout_bf16, gain.reshape(1, d))


def _ffn_kernel(x_ref, gpre_ref, gpost_ref, win_ref, wout_ref, out_ref, act_scr):
    x = x_ref[...]
    h = _rms(x, gpre_ref[...]).astype(BF16)
    dff = wout_ref.shape[0]
    for c in range(0, dff, MXU_TILE):
        gate = jnp.dot(h, win_ref[:, c:c + MXU_TILE], preferred_element_type=F32)
        up = jnp.dot(h, win_ref[:, dff + c:dff + c + MXU_TILE], preferred_element_type=F32)
        act_scr[:, c:c + MXU_TILE] = (gate * _sigmoid(gate) * up).astype(BF16)
    z = jnp.dot(act_scr[...], wout_ref[...], preferred_element_type=F32)
    out_ref[...] = x + _rms(z, gpost_ref[...])


def _ffn(x2d, gpre, gpost, win_bf16, wout_bf16, layer):
    m, d = x2d.shape
    tm = TOKEN_TILE
    dff = wout_bf16.shape[1]
    return pl.pallas_call(
        _ffn_kernel,
        grid=(m // tm,),
        in_specs=[pl.BlockSpec((tm, d), lambda i: (i, 0)), _resident((1, d)), _resident((1, d)),
                  _layer_resident(win_bf16.shape, layer), _layer_resident(wout_bf16.shape, layer)],
        out_specs=pl.BlockSpec((tm, d), lambda i: (i, 0)),
        out_shape=jax.ShapeDtypeStruct((m, d), F32),
        scratch_shapes=[pltpu.VMEM((tm, dff), BF16)],
        compiler_params=_params(("parallel",)),
        name="ffn",
    )(x2d, gpre.reshape(1, d), gpost.reshape(1, d), win_bf16, wout_bf16)


def _layer(layer, x3, big, norm_mix_pre, norm_mix_post, norm_ffn_pre, norm_ffn_post, rwkv_mu, rwkv_w0, rwkv_w_up,
           rwkv_a0, rwkv_a_up, rwkv_g_up, rwkv_k_k, rwkv_k_a, rwkv_r_k, rwkv_ln_w, rwkv_ln_b, conv_w):
    w_in, w_branch, w_out, w_ffn_in, w_ffn_out = big
    bsz, seq, d = x3.shape
    m = bsz * seq
    p_rwkv, p_conv, p_attn, p_gate = _inproj(x3.reshape(m, d), norm_mix_pre, w_in, layer)
    in3 = lambda t: t.reshape(bsz, seq, t.shape[-1])
    prep = _rwkv_prep(in3(p_rwkv), rwkv_mu, rwkv_w0, rwkv_w_up, rwkv_a0, rwkv_a_up, rwkv_g_up,
                      rwkv_k_k, rwkv_k_a, rwkv_r_k.reshape(-1))
    ya = _rwkv_scan(prep, rwkv_ln_w, rwkv_ln_b)
    attn = [_attn_group(in3(p_attn), gi, dil) for gi, (_, dil) in enumerate(ATTN_GROUPS)]
    x3 = _merge(x3, ya, in3(p_conv), attn, in3(p_gate), conv_w, w_branch, w_out, norm_mix_post, layer)
    x2 = _ffn(x3.reshape(m, d), norm_ffn_pre, norm_ffn_post, w_ffn_in, w_ffn_out, layer)
    return x2.reshape(bsz, seq, d)


def kernel(x, norm_mix_pre, norm_mix_post, norm_ffn_pre, norm_ffn_post, w_in, rwkv_mu, rwkv_w0, rwkv_w_up, rwkv_a0, rwkv_a_up, rwkv_g_up, rwkv_k_k, rwkv_k_a, rwkv_r_k, rwkv_ln_w, rwkv_ln_b, conv_w, w_branch, w_out, w_ffn_in, w_ffn_out):
    big = tuple(w.astype(BF16) for w in (w_in, w_branch, w_out, w_ffn_in, w_ffn_out))
    small = (norm_mix_pre, norm_mix_post, norm_ffn_pre, norm_ffn_post, rwkv_mu, rwkv_w0, rwkv_w_up,
             rwkv_a0, rwkv_a_up, rwkv_g_up, rwkv_k_k, rwkv_k_a, rwkv_r_k, rwkv_ln_w, rwkv_ln_b, conv_w)
    for layer in range(w_in.shape[0]):
        x = _layer(layer, x, big, *(p[layer] for p in small))
    return x
```

```python
import functools

import jax
import jax.numpy as jnp
from jax import lax
from jax.experimental import pallas as pl
from jax.experimental.pallas import tpu as pltpu

F32 = jnp.float32
BF16 = jnp.bfloat16

HEAD_DIM = 64
NORM_EPS = 1e-6
RWKV_HEADS = 8
RWKV_WIDTH = RWKV_HEADS * HEAD_DIM
LORA_DECAY = 64
LORA_AAA = 64
LORA_GATE = 128
RWKV_COLS = 3 * RWKV_WIDTH + LORA_DECAY + LORA_AAA + LORA_GATE
RWKV_GN_EPS = 64e-5
CONV_WIDTH = 512
CONV_COLS = 3 * CONV_WIDTH
ATTN_GROUPS = ((128, 1), (512, 4), (2048, 16))
HEADS_PER_GROUP = 4
ATTN_HEADS = HEADS_PER_GROUP * len(ATTN_GROUPS)
ATTN_WIDTH = ATTN_HEADS * HEAD_DIM
ATTN_COLS = 3 * ATTN_WIDTH
ATTN_OUT_WIDTH = HEADS_PER_GROUP * HEAD_DIM
ALIBI_MAX_EXP = 8.0
ATTN_BLK = 128
MASK_VALUE = -1e30

MXU_TILE = 256
CHUNK = 64
QUAD_LANES = MXU_TILE
SCAN_GROUP = 4
TOKEN_TILE = 512
VMEM_LIMIT = 56 * 1024 * 1024


def _params(sem):
    return pltpu.CompilerParams(dimension_semantics=sem, vmem_limit_bytes=VMEM_LIMIT)


def _resident(shape):
    zeros = (0,) * len(shape)
    return pl.BlockSpec(shape, lambda *_: zeros, pipeline_mode=pl.Buffered(1))


def _layer_resident(stacked_shape, layer):
    tail = tuple(stacked_shape[1:])
    zeros = (0,) * len(tail)
    return pl.BlockSpec((None,) + tail, lambda *_: (layer,) + zeros, pipeline_mode=pl.Buffered(1))


def _dot(a, b):
    return jnp.dot(a.astype(BF16), b.astype(BF16), preferred_element_type=F32)


def _dot_nt(a, b):
    return lax.dot_general(a.astype(BF16), b.astype(BF16), (((1,), (1,)), ((), ())), preferred_element_type=F32)


def _dot_tn(a, b):
    return lax.dot_general(a.astype(BF16), b.astype(BF16), (((0,), (0,)), ((), ())), preferred_element_type=F32)


def _split3(x):
    hi = x.astype(BF16)
    r1 = x - hi.astype(F32)
    mid = r1.astype(BF16)
    lo = (r1 - mid.astype(F32)).astype(BF16)
    return hi, mid, lo


def _dot_f32_lhs(a, b01, terms=3):
    return sum(jnp.dot(x, b01, preferred_element_type=F32) for x in _split3(a)[:terms])


def _dot_x3(a, b):
    ah = a.astype(BF16)
    al = (a - ah.astype(F32)).astype(BF16)
    bh = b.astype(BF16)
    bl = (b - bh.astype(F32)).astype(BF16)
    d = lambda x, y: jnp.dot(x, y, preferred_element_type=F32)
    return d(ah, bh) + d(ah, bl) + d(al, bh)


def _rms(x, gain):
    return x * lax.rsqrt(jnp.mean(x * x, axis=-1, keepdims=True) + NORM_EPS) * gain


def _sigmoid(z):
    return 0.5 * jnp.tanh(0.5 * z) + 0.5


def _softplus(z):
    return jnp.maximum(z, 0.0) + jnp.log(1.0 + jnp.exp(-jnp.abs(z)))


def _shift_rows(cur, prev_tail, n):
    rows = lax.broadcasted_iota(jnp.int32, cur.shape, 0)
    out = pltpu.roll(cur, n, axis=0)
    for i in range(n):
        out = jnp.where(rows == i, prev_tail[8 - n + i:8 - n + i + 1, :], out)
    return out


def _inproj_kernel(x_ref, gain_ref, w_ref, o_rwkv, o_conv, o_attn, o_gate):
    h = _rms(x_ref[...], gain_ref[...]).astype(BF16)
    lo = 0
    for o_ref in (o_rwkv, o_conv, o_attn, o_gate):
        width = o_ref.shape[-1]
        for c in range(0, width, MXU_TILE):
            o_ref[:, c:c + MXU_TILE] = jnp.dot(h, w_ref[:, lo + c:lo + c + MXU_TILE],
                                               preferred_element_type=F32).astype(o_ref.dtype)
        lo += width


def _inproj(x2d, gain, w_in_bf16, layer):
    m, d = x2d.shape
    tm = TOKEN_TILE
    widths = (RWKV_COLS, CONV_COLS, ATTN_COLS, w_in_bf16.shape[2] - RWKV_COLS - CONV_COLS - ATTN_COLS)
    dtypes = (F32, BF16, BF16, BF16)
    return pl.pallas_call(
        _inproj_kernel,
        grid=(m // tm,),
        in_specs=[pl.BlockSpec((tm, d), lambda i: (i, 0)),
                  _resident((1, d)),
                  _layer_resident(w_in_bf16.shape, layer)],
        out_specs=[pl.BlockSpec((tm, w), lambda i: (i, 0)) for w in widths],
        out_shape=[jax.ShapeDtypeStruct((m, w), dt) for w, dt in zip(widths, dtypes)],
        compiler_params=_params(("parallel",)),
        name="inproj",
    )(x2d, gain.reshape(1, d), w_in_bf16)


def _rwkv_prep_kernel(p_ref, prev_ref, mu_ref, w0_ref, wup_ref, a0_ref, aup_ref, gup_ref, kk_ref, ka_ref,
                      rk_ref, bd_ref, ltri_ref,
                      rt_ref, at_ref, bt_ref, kt_ref, bh_ref, kh_ref, v_ref, gam_ref, g_ref, bonus_ref):
    first = pl.program_id(1) == 0
    p = p_ref[0]
    prev = jnp.where(first, 0.0, prev_ref[0])
    xm = p + (_shift_rows(p, prev, 1) - p) * mu_ref[...]
    w = RWKV_WIDTH
    r, k, v = xm[:, 0:w], xm[:, w:2 * w], xm[:, 2 * w:3 * w]
    wa = xm[:, 3 * w:3 * w + LORA_DECAY + LORA_AAA]
    gl = xm[:, 3 * w + LORA_DECAY + LORA_AAA:]
    w_log = -_softplus(-(w0_ref[...] + _dot_x3(jnp.tanh(wa), wup_ref[...]))) - 0.5
    lw = -jnp.exp(w_log)
    a_lr = _sigmoid(a0_ref[...] + _dot(wa, aup_ref[...]))
    g = _dot(_sigmoid(gl), gup_ref[...])
    bd = bd_ref[...]
    kk = k * kk_ref[...]
    kk = kk * lax.rsqrt(jnp.maximum(_dot_f32_lhs(kk * kk, bd, terms=2), 1e-12))
    k2 = k * (1.0 + (a_lr - 1.0) * ka_ref[...])
    bonus = _dot(r * k2 * rk_ref[...], bd) * v
    a = -kk
    b = kk * a_lr
    ltri = ltri_ref[...]
    parts = _split3(lw)
    cums, tots = [], []
    for c in range(p.shape[0] // CHUNK):
        rc = slice(c * CHUNK, (c + 1) * CHUNK)
        cum_c = sum(jnp.dot(ltri, x[rc], preferred_element_type=F32) for x in parts)
        cums.append(cum_c)
        tots.append(cum_c[CHUNK - 1:CHUNK])
    cum = jnp.concatenate(cums, axis=0)
    tot = jnp.concatenate([jnp.broadcast_to(t, (CHUNK, t.shape[1])) for t in tots], axis=0)
    e_neg = jnp.exp(-cum)
    e_rem = jnp.exp(tot - cum)
    rt_ref[0] = (r * jnp.exp(cum)).astype(BF16)
    at_ref[0] = (a * jnp.exp(cum - lw)).astype(BF16)
    bt_ref[0] = (b * e_neg).astype(BF16)
    kt_ref[0] = (k2 * e_neg).astype(BF16)
    bh_ref[0] = (b * e_rem).astype(BF16)
    kh_ref[0] = (k2 * e_rem).astype(BF16)
    v_ref[0] = v.astype(BF16)
    gam_ref[0] = jnp.exp(jnp.concatenate(tots, axis=0))
    g_ref[0] = g
    bonus_ref[0] = bonus


def _rwkv_prep(p3, mu, w0, w_up, a0, a_up, g_up, k_k, k_a, r_k):
    bsz, seq, cols = p3.shape
    tm = TOKEN_TILE
    w = RWKV_WIDTH
    nck = tm // CHUNK
    lora = LORA_DECAY + LORA_AAA
    wup_pad = jnp.zeros((lora, w), F32).at[:LORA_DECAY].set(w_up)
    aup_pad = jnp.zeros((lora, w), F32).at[LORA_DECAY:].set(a_up)
    hid = jnp.arange(w) // HEAD_DIM
    bd = (hid[:, None] == hid[None, :]).astype(BF16)
    t = jnp.arange(CHUNK)
    ltri = (t[:, None] >= t[None, :]).astype(BF16)
    row = lambda vec: vec.reshape(1, -1)
    tok = lambda width: pl.BlockSpec((1, tm, width), lambda b, j: (b, j, 0))
    ins = [tok(cols),
           pl.BlockSpec((1, 8, cols), lambda b, j: (b, jnp.maximum(j * (tm // 8) - 1, 0), 0)),
           _resident((1, cols)), _resident((1, w)), _resident((lora, w)), _resident((1, w)),
           _resident((lora, w)), _resident((LORA_GATE, w)), _resident((1, w)), _resident((1, w)),
           _resident((1, w)), _resident((w, w)), _resident((CHUNK, CHUNK))]
    outs = [tok(w)] * 7 + [pl.BlockSpec((1, nck, w), lambda b, j: (b, j, 0)), tok(w), tok(w)]
    shapes = ([jax.ShapeDtypeStruct((bsz, seq, w), BF16)] * 7
              + [jax.ShapeDtypeStruct((bsz, seq // CHUNK, w), F32)]
              + [jax.ShapeDtypeStruct((bsz, seq, w), F32)] * 2)
    return pl.pallas_call(
        _rwkv_prep_kernel,
        grid=(bsz, seq // tm),
        in_specs=ins, out_specs=outs, out_shape=shapes,
        compiler_params=_params(("parallel", "parallel")),
        name="rwkv_prep",
    )(p3, p3, row(mu), row(w0), wup_pad, row(a0), aup_pad.astype(BF16), g_up.astype(BF16), row(k_k), row(k_a),
      row(r_k), bd, ltri)


def _head_tile(z, head_diag):
    z = z.astype(BF16)
    reps = head_diag.shape[0] // z.shape[0]
    return jnp.where(head_diag, jnp.concatenate([z] * reps, axis=0), jnp.zeros(head_diag.shape, BF16))


def _rwkv_scan_kernel(rt_ref, at_ref, bt_ref, kt_ref, bh_ref, kh_ref, v_ref, gam_ref, g_ref, bonus_ref,
                      lnw_ref, lnb_ref, bd_ref, o_ref, s_scr, y_scr):
    @pl.when(pl.program_id(1) == 0)
    def _():
        s_scr[...] = jnp.zeros_like(s_scr)

    quad = QUAD_LANES
    r1 = lax.broadcasted_iota(jnp.int32, (quad, quad), 0)
    c1 = lax.broadcasted_iota(jnp.int32, (quad, quad), 1)
    head_diag = (r1 // CHUNK) == (c1 // HEAD_DIM)
    ts = lax.broadcasted_iota(jnp.int32, (CHUNK, quad), 0)
    ss = lax.broadcasted_iota(jnp.int32, (CHUNK, quad), 1) % HEAD_DIM
    strict = ss < ts
    incl = ss <= ts
    eye = (ss == ts).astype(F32)
    nquad = rt_ref.shape[2] // quad
    units = [(c, q) for c in range(SCAN_GROUP) for q in range(nquad)]
    tile = functools.partial(_head_tile, head_diag=head_diag)
    lane_cat = lambda x, y: jnp.concatenate([x, y], axis=1)
    lane_head = lax.broadcasted_iota(jnp.int32, (CHUNK, quad), 1) // HEAD_DIM

    def fold(t):
        out = t[:CHUNK]
        for h in range(1, quad // HEAD_DIM):
            out = jnp.where(lane_head == h, t[h * CHUNK:(h + 1) * CHUNK], out)
        return out

    row_cat = lambda x, y: jnp.concatenate([x, y], axis=0)

    def group_body(gi, carry):
        base = gi * (SCAN_GROUP * CHUNK)
        ops = []
        for c, q in units:
            rows = pl.ds(pl.multiple_of(base + c * CHUNK, CHUNK), CHUNK)
            lanes = slice(q * quad, (q + 1) * quad)
            ops.append([ref[0, rows, lanes] for ref in (rt_ref, at_ref, bt_ref, kt_ref, bh_ref, kh_ref, v_ref)])
        rt, at, bt, kt, bh, kh, v = (list(x) for x in zip(*ops))
        rng = range(len(units))
        gram = [_dot_nt(row_cat(at[i], rt[i]), row_cat(tile(bt[i]), tile(kt[i]))) for i in rng]
        pw = [jnp.where(strict, gram[i][:CHUNK, :quad], 0.0) for i in rng]
        akrk = [row_cat(jnp.where(strict, gram[i][:CHUNK, quad:], 0.0),
                        jnp.where(incl, gram[i][CHUNK:, quad:], 0.0)).astype(BF16) for i in rng]
        a_rb = [jnp.where(incl, gram[i][CHUNK:, :quad], 0.0).astype(BF16) for i in rng]
        t_inv = [eye + pw[i] for i in rng]
        pw = [_dot(pw[i], tile(pw[i])) for i in rng]
        for _ in range(4):
            both = [_dot(row_cat(t_inv[i], pw[i]), tile(pw[i])) for i in rng]
            t_inv = [t_inv[i] + both[i][:CHUNK] for i in rng]
            pw = [both[i][CHUNK:] for i in rng]
        t_inv = [t_inv[i] + _dot(t_inv[i], tile(pw[i])) for i in rng]
        avrk = [_dot(akrk[i], tile(v[i])) for i in rng]
        uw = [_dot(t_inv[i], lane_cat(tile(avrk[i][:CHUNK]), tile(at[i]))) for i in rng]
        u = [x[:, :quad] for x in uw]
        wm = [x[:, quad:] for x in uw]
        qy = [_dot(a_rb[i], lane_cat(tile(wm[i]), tile(u[i]))) for i in rng]
        qh = [rt[i].astype(F32) + qy[i][:, :quad] for i in rng]
        yi = [qy[i][:, quad:] + avrk[i][CHUNK:] for i in rng]
        mp = [jnp.where(head_diag, _dot_tn(wm[i], bh[i]), 0.0).astype(BF16) for i in rng]
        nps = [fold(_dot_tn(row_cat(u[i].astype(BF16), v[i]), row_cat(bh[i], kh[i]))) for i in rng]
        st = [s_scr[q] for q in range(nquad)]
        for i, (c, q) in enumerate(units):
            lanes = slice(q * quad, (q + 1) * quad)
            y_scr[pl.ds(pl.multiple_of(base + c * CHUNK, CHUNK), CHUNK), lanes] = _dot_nt(qh[i], tile(st[q])) + yi[i]
            gam = gam_ref[0, pl.ds(gi * SCAN_GROUP + c, 1), lanes]
            st[q] = st[q] * gam + _dot(st[q], mp[i]) + nps[i]
        for q in range(nquad):
            s_scr[q] = st[q]
        return carry

    lax.fori_loop(0, rt_ref.shape[1] // (SCAN_GROUP * CHUNK), group_body, 0)

    y = y_scr[...]
    bd = bd_ref[...]
    inv_n = 1.0 / HEAD_DIM
    mean = _dot_f32_lhs(y, bd, terms=2) * inv_n
    d = y - mean
    var = _dot(d * d, bd) * inv_n
    yn = d * lax.rsqrt(var + RWKV_GN_EPS) * lnw_ref[...] + lnb_ref[...]
    o_ref[0] = ((yn + bonus_ref[0]) * g_ref[0]).astype(o_ref.dtype)


def _rwkv_scan(prep, ln_w, ln_b):
    rt = prep[0]
    bsz, seq, w = rt.shape
    tm = TOKEN_TILE
    nck = tm // CHUNK
    hid = jnp.arange(w) // HEAD_DIM
    bd = (hid[:, None] == hid[None, :]).astype(BF16)
    tok = pl.BlockSpec((1, tm, w), lambda b, j: (b, j, 0))
    ins = [tok] * 7 + [pl.BlockSpec((1, nck, w), lambda b, j: (b, j, 0)), tok, tok,
                       _resident((1, w)), _resident((1, w)), _resident((w, w))]
    return pl.pallas_call(
        _rwkv_scan_kernel,
        grid=(bsz, seq // tm),
        in_specs=ins, out_specs=tok,
        out_shape=jax.ShapeDtypeStruct((bsz, seq, w), BF16),
        scratch_shapes=[pltpu.VMEM((w // QUAD_LANES, HEAD_DIM, QUAD_LANES), F32), pltpu.VMEM((tm, w), F32)],
        compiler_params=_params(("parallel", "arbitrary")),
        name="rwkv_scan",
    )(*prep, ln_w.reshape(1, w), ln_b.reshape(1, w), bd)


ATTN_PLANE = 128
ATTN_UNROLL = 2


def _attn_kernel(q_ref, kp_ref, kc_ref, vp_ref, vc_ref, bias_ref, o_ref, l_ref,
                 q_scr, k_scr, v_scr, o_scr, l_scr, *, dilation):
    blk = ATTN_BLK
    span = blk * dilation
    tile = q_ref.shape[1]
    nplane = q_ref.shape[2] // ATTN_PLANE
    first = pl.program_id(1) == 0
    for j in range(nplane):
        ls = slice(j * ATTN_PLANE, (j + 1) * ATTN_PLANE)
        q_scr[j] = q_ref[0, :, ls].astype(F32) * (HEAD_DIM ** -0.5)
        k_scr[j, 0:span] = kp_ref[0, :, ls].astype(F32)
        k_scr[j, span:] = kc_ref[0, :, ls].astype(F32)
        v_scr[j, 0:span] = vp_ref[0, :, ls].astype(F32)
        v_scr[j, span:] = vc_ref[0, :, ls].astype(F32)
    col = lax.broadcasted_iota(jnp.int32, (blk, 2 * blk), 1)
    lane = lax.broadcasted_iota(jnp.int32, (blk, ATTN_PLANE), 1)
    per_plane = ATTN_PLANE // HEAD_DIM
    mine = [(lane // HEAD_DIM) == hh for hh in range(per_plane)]
    heads = [(j, hh) for j in range(nplane) for hh in range(per_plane)]

    def rows_of(idx):
        start = (idx % dilation) + (idx // dilation) * span
        if dilation == 1:
            start = pl.multiple_of(start, blk)
            return pl.ds(start, blk), pl.ds(start, 2 * blk)
        return pl.ds(start, blk, stride=dilation), pl.ds(start, 2 * blk, stride=dilation)

    def pair_body(it, carry):
        blocks = [it * ATTN_UNROLL + b for b in range(ATTN_UNROLL)]
        rows = [rows_of(idx) for idx in blocks]
        exists = [jnp.logical_or(jnp.logical_or(jnp.logical_not(first), idx >= dilation), col >= blk)
                  for idx in blocks]
        qp = [[q_scr[j, qr, :].astype(BF16) for j in range(nplane)] for qr, _ in rows]
        kp = [[k_scr[j, kr, :].astype(BF16) for j in range(nplane)] for _, kr in rows]
        vp = [[v_scr[j, kr, :].astype(BF16) for j in range(nplane)] for _, kr in rows]
        streams = [(b, i) for b in range(ATTN_UNROLL) for i in range(len(heads))]
        s = [_dot_nt(jnp.where(mine[heads[i][1]], qp[b][heads[i][0]], jnp.zeros_like(qp[b][heads[i][0]])),
                     kp[b][heads[i][0]]) for b, i in streams]
        s = [jnp.where(exists[b], s[n] + bias_ref[i], MASK_VALUE) for n, (b, i) in enumerate(streams)]
        m = [jnp.max(x, axis=-1, keepdims=True) for x in s]
        pexp = [jnp.exp(x - mx) for x, mx in zip(s, m)]
        l = [jnp.sum(x, axis=-1, keepdims=True) for x in pexp]
        o = [_dot(pexp[n], vp[b][heads[i][0]]) / l[n] for n, (b, i) in enumerate(streams)]
        lse = [mx + jnp.log(lx) for mx, lx in zip(m, l)]
        for b in range(ATTN_UNROLL):
            for j in range(nplane):
                n0 = b * len(heads) + j * per_plane
                o_pair = o[n0]
                l_pair = jnp.broadcast_to(lse[n0], o_pair.shape)
                for hh in range(1, per_plane):
                    o_pair = jnp.where(mine[hh], o[n0 + hh], o_pair)
                    l_pair = jnp.where(mine[hh], lse[n0 + hh], l_pair)
                o_scr[j, rows[b][0], :] = o_pair
                l_scr[j, rows[b][0], :] = l_pair
        return carry

    lax.fori_loop(0, tile // (blk * ATTN_UNROLL), pair_body, 0)
    for j in range(nplane):
        ls = slice(j * ATTN_PLANE, (j + 1) * ATTN_PLANE)
        o_ref[0, :, ls] = o_scr[j]
        l_ref[0, :, ls] = l_scr[j]


def _attn_bias(group, dilation):
    blk = ATTN_BLK
    heads = jnp.arange(group * HEADS_PER_GROUP + 1, (group + 1) * HEADS_PER_GROUP + 1, dtype=F32)
    slopes = jnp.exp2(-ALIBI_MAX_EXP * heads / ATTN_HEADS)
    steps = blk + jnp.arange(blk)[:, None] - jnp.arange(2 * blk)[None, :]
    in_window = (steps >= 0) & (steps <= blk)
    alibi = -slopes[:, None, None] * (steps * dilation).astype(F32)[None]
    return jnp.where(in_window[None], alibi, MASK_VALUE)


def _attn_group(p_attn3, group, dilation):
    bsz, seq, cols = p_attn3.shape
    blk = ATTN_BLK
    gw = ATTN_OUT_WIDTH
    span = blk * dilation
    tile = max(span, TOKEN_TILE)
    nplane = gw // ATTN_PLANE
    per_kind = ATTN_WIDTH // gw

    def cur(kind):
        return pl.BlockSpec((1, tile, gw), lambda b, n: (b, n, kind * per_kind + group))

    def prev(kind):
        return pl.BlockSpec((1, span, gw),
                            lambda b, n: (b, jnp.maximum(n * (tile // span) - 1, 0), kind * per_kind + group))

    out_spec = pl.BlockSpec((1, tile, gw), lambda b, n: (b, n, 0))
    plane = lambda rows: pltpu.VMEM((nplane, rows, ATTN_PLANE), F32)
    return pl.pallas_call(
        functools.partial(_attn_kernel, dilation=dilation),
        grid=(bsz, seq // tile),
        in_specs=[cur(0), prev(1), cur(1), prev(2), cur(2), _resident((HEADS_PER_GROUP, blk, 2 * blk))],
        out_specs=[out_spec, out_spec],
        out_shape=[jax.ShapeDtypeStruct((bsz, seq, gw), F32)] * 2,
        scratch_shapes=[plane(tile), plane(span + tile), plane(span + tile), plane(tile), plane(tile)],
        compiler_params=_params(("parallel", "arbitrary")),
        name=f"attn_g{group}",
    )(p_attn3, p_attn3, p_attn3, p_attn3, p_attn3, _attn_bias(group, dilation))


def _merge_kernel(x_ref, ya_ref, pc_ref, pcprev_ref, o0_ref, o1_ref, o2_ref, l0_ref, l1_ref, l2_ref,
                  gate_ref, convw_ref, wb_ref, wout_ref, gain_ref, out_ref):
    first = pl.program_id(1) == 0
    cw = CONV_WIDTH
    pc = pc_ref[0].astype(F32)
    pprev = pcprev_ref[0].astype(F32)
    u = pc[:, cw:2 * cw] * pc[:, 2 * cw:]
    uprev = jnp.where(first, 0.0, pprev[:, cw:2 * cw] * pprev[:, 2 * cw:])
    cwt = convw_ref[...]
    yb = pc[:, :cw] * (cwt[0:1] * _shift_rows(u, uprev, 2) + cwt[1:2] * _shift_rows(u, uprev, 1) + cwt[2:3] * u)

    l0, l1, l2 = l0_ref[0], l1_ref[0], l2_ref[0]
    mx = jnp.maximum(jnp.maximum(l0, l1), l2)
    e0, e1, e2 = jnp.exp(l0 - mx), jnp.exp(l1 - mx), jnp.exp(l2 - mx)
    yc = (e0 * o0_ref[0] + e1 * o1_ref[0] + e2 * o2_ref[0]) / (e0 + e1 + e2)

    d = x_ref.shape[-1]
    gate = gate_ref[0].astype(F32)
    ra, rb = RWKV_WIDTH, RWKV_WIDTH + CONV_WIDTH
    merged = (_sigmoid(gate[:, :d]) * _dot(ya_ref[0], wb_ref[:ra])
              + _sigmoid(gate[:, d:2 * d]) * _dot(yb, wb_ref[ra:rb])
              + _sigmoid(gate[:, 2 * d:]) * _dot(yc, wb_ref[rb:]))
    out_ref[0] = x_ref[0] + _rms(_dot(merged, wout_ref[...]), gain_ref[...])


def _merge(x3, ya, p_conv3, attn, p_gate3, conv_w, wb_bf16, wout_bf16, gain, layer):
    bsz, seq, d = x3.shape
    tm = TOKEN_TILE
    tok = lambda width: pl.BlockSpec((1, tm, width), lambda b, j: (b, j, 0))
    (o0, l0), (o1, l1), (o2, l2) = attn
    gw = ATTN_OUT_WIDTH
    ins = [tok(d), tok(RWKV_WIDTH), tok(CONV_COLS),
           pl.BlockSpec((1, 8, CONV_COLS), lambda b, j: (b, jnp.maximum(j * (tm // 8) - 1, 0), 0)),
           tok(gw), tok(gw), tok(gw), tok(gw), tok(gw), tok(gw), tok(p_gate3.shape[-1]),
           _resident(conv_w.shape), _layer_resident(wb_bf16.shape, layer), _layer_resident(wout_bf16.shape, layer),
           _resident((1, d))]
    return pl.pallas_call(
        _merge_kernel,
        grid=(bsz, seq // tm),
        in_specs=ins, out_specs=tok(d),
        out_shape=jax.ShapeDtypeStruct((bsz, seq, d), F32),
        compiler_params=_params(("parallel", "parallel")),
        name="merge",
    )(x3, ya, p_conv3, p_conv3, o0, o1, o2, l0, l1, l2, p_gate3, conv_w, wb_bf16, wout_bf16, gain.reshape(1, d))


def _ffn_kernel(x_ref, gpre_ref, gpost_ref, win_ref, wout_ref, out_ref, act_scr):
    x = x_ref[...]
    h = _rms(x, gpre_ref[...]).astype(BF16)
    dff = wout_ref.shape[0]
    for c in range(0, dff, MXU_TILE):
        gate = jnp.dot(h, win_ref[:, c:c + MXU_TILE], preferred_element_type=F32)
        up = jnp.dot(h, win_ref[:, dff + c:dff + c + MXU_TILE], preferred_element_type=F32)
        act_scr[:, c:c + MXU_TILE] = (gate * _sigmoid(gate) * up).astype(BF16)
    z = jnp.dot(act_scr[...], wout_ref[...], preferred_element_type=F32)
    out_ref[...] = x + _rms(z, gpost_ref[...])


def _ffn(x2d, gpre, gpost, win_bf16, wout_bf16, layer):
    m, d = x2d.shape
    tm = TOKEN_TILE
    dff = wout_bf16.shape[1]
    return pl.pallas_call(
        _ffn_kernel,
        grid=(m // tm,),
        in_specs=[pl.BlockSpec((tm, d), lambda i: (i, 0)), _resident((1, d)), _resident((1, d)),
                  _layer_resident(win_bf16.shape, layer), _layer_resident(wout_bf16.shape, layer)],
        out_specs=pl.BlockSpec((tm, d), lambda i: (i, 0)),
        out_shape=jax.ShapeDtypeStruct((m, d), F32),
        scratch_shapes=[pltpu.VMEM((tm, dff), BF16)],
        compiler_params=_params(("parallel",)),
        name="ffn",
    )(x2d, gpre.reshape(1, d), gpost.reshape(1, d), win_bf16, wout_bf16)


def _layer(layer, x3, big, norm_mix_pre, norm_mix_post, norm_ffn_pre, norm_ffn_post, rwkv_mu, rwkv_w0, rwkv_w_up,
           rwkv_a0, rwkv_a_up, rwkv_g_up, rwkv_k_k, rwkv_k_a, rwkv_r_k, rwkv_ln_w, rwkv_ln_b, conv_w):
    w_in, w_branch, w_out, w_ffn_in, w_ffn_out = big
    bsz, seq, d = x3.shape
    m = bsz * seq
    p_rwkv, p_conv, p_attn, p_gate = _inproj(x3.reshape(m, d), norm_mix_pre, w_in, layer)
    in3 = lambda t: t.reshape(bsz, seq, t.shape[-1])
    prep = _rwkv_prep(in3(p_rwkv), rwkv_mu, rwkv_w0, rwkv_w_up, rwkv_a0, rwkv_a_up, rwkv_g_up,
                      rwkv_k_k, rwkv_k_a, rwkv_r_k.reshape(-1))
    ya = _rwkv_scan(prep, rwkv_ln_w, rwkv_ln_b)
    attn = [_attn_group(in3(p_attn), gi, dil) for gi, (_, dil) in enumerate(ATTN_GROUPS)]
    x3 = _merge(x3, ya, in3(p_conv), attn, in3(p_gate), conv_w, w_branch, w_out, norm_mix_post, layer)
    x2 = _ffn(x3.reshape(m, d), norm_ffn_pre, norm_ffn_post, w_ffn_in, w_ffn_out, layer)
    return x2.reshape(bsz, seq, d)


def kernel(x, norm_mix_pre, norm_mix_post, norm_ffn_pre, norm_ffn_post, w_in, rwkv_mu, rwkv_w0, rwkv_w_up, rwkv_a0, rwkv_a_up, rwkv_g_up, rwkv_k_k, rwkv_k_a, rwkv_r_k, rwkv_ln_w, rwkv_ln_b, conv_w, w_branch, w_out, w_ffn_in, w_ffn_out):
    big = tuple(w.astype(BF16) for w in (w_in, w_branch, w_out, w_ffn_in, w_ffn_out))
    small = (norm_mix_pre, norm_mix_post, norm_ffn_pre, norm_ffn_post, rwkv_mu, rwkv_w0, rwkv_w_up,
             rwkv_a0, rwkv_a_up, rwkv_g_up, rwkv_k_k, rwkv_k_a, rwkv_r_k, rwkv_ln_w, rwkv_ln_b, conv_w)
    for layer in range(w_in.shape[0]):
        x = _layer(layer, x, big, *(p[layer] for p in small))
    return x
```

```python
import functools

import jax
import jax.numpy as jnp
from jax import lax
from jax.experimental import pallas as pl
from jax.experimental.pallas import tpu as pltpu

F32 = jnp.float32
BF16 = jnp.bfloat16

HEAD_DIM = 64
NORM_EPS = 1e-6
RWKV_HEADS = 8
RWKV_WIDTH = RWKV_HEADS * HEAD_DIM
LORA_DECAY = 64
LORA_AAA = 64
LORA_GATE = 128
RWKV_COLS = 3 * RWKV_WIDTH + LORA_DECAY + LORA_AAA + LORA_GATE
RWKV_GN_EPS = 64e-5
CONV_WIDTH = 512
CONV_COLS = 3 * CONV_WIDTH
ATTN_GROUPS = ((128, 1), (512, 4), (2048, 16))
HEADS_PER_GROUP = 4
ATTN_HEADS = HEADS_PER_GROUP * len(ATTN_GROUPS)
ATTN_WIDTH = ATTN_HEADS * HEAD_DIM
ATTN_COLS = 3 * ATTN_WIDTH
ATTN_OUT_WIDTH = HEADS_PER_GROUP * HEAD_DIM
ALIBI_MAX_EXP = 8.0
ATTN_BLK = 128
MASK_VALUE = -1e30

MXU_TILE = 256
CHUNK = 64
QUAD_LANES = MXU_TILE
SCAN_GROUP = 8
TOKEN_TILE = 512
FFN_TILE = 1024
VMEM_LIMIT = 56 * 1024 * 1024


def _params(sem):
    return pltpu.CompilerParams(dimension_semantics=sem, vmem_limit_bytes=VMEM_LIMIT)


def _resident(shape):
    zeros = (0,) * len(shape)
    return pl.BlockSpec(shape, lambda *_: zeros, pipeline_mode=pl.Buffered(1))


def _layer_resident(stacked_shape, layer):
    tail = tuple(stacked_shape[1:])
    zeros = (0,) * len(tail)
    return pl.BlockSpec((None,) + tail, lambda *_: (layer,) + zeros, pipeline_mode=pl.Buffered(1))


def _dot(a, b):
    return jnp.dot(a.astype(BF16), b.astype(BF16), preferred_element_type=F32)


def _dot_nt(a, b):
    return lax.dot_general(a.astype(BF16), b.astype(BF16), (((1,), (1,)), ((), ())), preferred_element_type=F32)


def _dot_tn(a, b):
    return lax.dot_general(a.astype(BF16), b.astype(BF16), (((0,), (0,)), ((), ())), preferred_element_type=F32)


def _split3(x):
    hi = x.astype(BF16)
    r1 = x - hi.astype(F32)
    mid = r1.astype(BF16)
    lo = (r1 - mid.astype(F32)).astype(BF16)
    return hi, mid, lo


def _dot_f32_lhs(a, b01, terms=3):
    return sum(jnp.dot(x, b01, preferred_element_type=F32) for x in _split3(a)[:terms])


def _dot_x3(a, b):
    ah = a.astype(BF16)
    al = (a - ah.astype(F32)).astype(BF16)
    bh = b.astype(BF16)
    bl = (b - bh.astype(F32)).astype(BF16)
    d = lambda x, y: jnp.dot(x, y, preferred_element_type=F32)
    return d(ah, bh) + d(ah, bl) + d(al, bh)


def _rms(x, gain):
    return x * lax.rsqrt(jnp.mean(x * x, axis=-1, keepdims=True) + NORM_EPS) * gain


def _sigmoid(z):
    return 0.5 * jnp.tanh(0.5 * z) + 0.5


def _softplus(z):
    return jnp.maximum(z, 0.0) + jnp.log(1.0 + jnp.exp(-jnp.abs(z)))


def _shift_rows(cur, prev_tail, n):
    rows = lax.broadcasted_iota(jnp.int32, cur.shape, 0)
    out = pltpu.roll(cur, n, axis=0)
    for i in range(n):
        out = jnp.where(rows == i, prev_tail[8 - n + i:8 - n + i + 1, :], out)
    return out


def _inproj_kernel(x_ref, gain_ref, w_ref, o_rwkv, o_conv, o_attn, o_gate):
    h = _rms(x_ref[...], gain_ref[...]).astype(BF16)
    lo = 0
    for o_ref in (o_rwkv, o_conv, o_attn, o_gate):
        width = o_ref.shape[-1]
        for c in range(0, width, MXU_TILE):
            o_ref[:, c:c + MXU_TILE] = jnp.dot(h, w_ref[:, lo + c:lo + c + MXU_TILE],
                                               preferred_element_type=F32).astype(o_ref.dtype)
        lo += width


def _inproj(x2d, gain, w_in_bf16, layer):
    m, d = x2d.shape
    tm = TOKEN_TILE
    widths = (RWKV_COLS, CONV_COLS, ATTN_COLS, w_in_bf16.shape[2] - RWKV_COLS - CONV_COLS - ATTN_COLS)
    dtypes = (F32, BF16, BF16, BF16)
    return pl.pallas_call(
        _inproj_kernel,
        grid=(m // tm,),
        in_specs=[pl.BlockSpec((tm, d), lambda i: (i, 0)),
                  _resident((1, d)),
                  _layer_resident(w_in_bf16.shape, layer)],
        out_specs=[pl.BlockSpec((tm, w), lambda i: (i, 0)) for w in widths],
        out_shape=[jax.ShapeDtypeStruct((m, w), dt) for w, dt in zip(widths, dtypes)],
        compiler_params=_params(("parallel",)),
        name="inproj",
    )(x2d, gain.reshape(1, d), w_in_bf16)


def _rwkv_prep_kernel(p_ref, prev_ref, mu_ref, w0_ref, wup_ref, a0_ref, aup_ref, gup_ref, kk_ref, ka_ref,
                      rk_ref, bd_ref, ltri_ref,
                      rt_ref, at_ref, bt_ref, kt_ref, bh_ref, kh_ref, v_ref, gam_ref, g_ref, bonus_ref):
    first = pl.program_id(1) == 0
    p = p_ref[0]
    prev = jnp.where(first, 0.0, prev_ref[0])
    xm = p + (_shift_rows(p, prev, 1) - p) * mu_ref[...]
    w = RWKV_WIDTH
    r, k, v = xm[:, 0:w], xm[:, w:2 * w], xm[:, 2 * w:3 * w]
    wa = xm[:, 3 * w:3 * w + LORA_DECAY + LORA_AAA]
    gl = xm[:, 3 * w + LORA_DECAY + LORA_AAA:]
    w_log = -_softplus(-(w0_ref[...] + _dot_x3(jnp.tanh(wa), wup_ref[...]))) - 0.5
    lw = -jnp.exp(w_log)
    a_lr = _sigmoid(a0_ref[...] + _dot(wa, aup_ref[...]))
    g = _dot(_sigmoid(gl), gup_ref[...])
    bd = bd_ref[...]
    kk = k * kk_ref[...]
    kk = kk * lax.rsqrt(jnp.maximum(_dot_f32_lhs(kk * kk, bd, terms=2), 1e-12))
    k2 = k * (1.0 + (a_lr - 1.0) * ka_ref[...])
    bonus = _dot(r * k2 * rk_ref[...], bd) * v
    a = -kk
    b = kk * a_lr
    ltri = ltri_ref[...]
    parts = _split3(lw)
    cums, tots = [], []
    for c in range(p.shape[0] // CHUNK):
        rc = slice(c * CHUNK, (c + 1) * CHUNK)
        cum_c = sum(jnp.dot(ltri, x[rc], preferred_element_type=F32) for x in parts)
        cums.append(cum_c)
        tots.append(cum_c[CHUNK - 1:CHUNK])
    cum = jnp.concatenate(cums, axis=0)
    tot = jnp.concatenate([jnp.broadcast_to(t, (CHUNK, t.shape[1])) for t in tots], axis=0)
    e_neg = jnp.exp(-cum)
    e_rem = jnp.exp(tot - cum)
    rt_ref[0] = (r * jnp.exp(cum)).astype(BF16)
    at_ref[0] = (a * jnp.exp(cum - lw)).astype(BF16)
    bt_ref[0] = (b * e_neg).astype(BF16)
    kt_ref[0] = (k2 * e_neg).astype(BF16)
    bh_ref[0] = (b * e_rem).astype(BF16)
    kh_ref[0] = (k2 * e_rem).astype(BF16)
    v_ref[0] = v.astype(BF16)
    gam_ref[0] = jnp.exp(jnp.concatenate(tots, axis=0))
    g_ref[0] = g
    bonus_ref[0] = bonus


def _rwkv_prep(p3, mu, w0, w_up, a0, a_up, g_up, k_k, k_a, r_k):
    bsz, seq, cols = p3.shape
    tm = TOKEN_TILE
    w = RWKV_WIDTH
    nck = tm // CHUNK
    lora = LORA_DECAY + LORA_AAA
    wup_pad = jnp.zeros((lora, w), F32).at[:LORA_DECAY].set(w_up)
    aup_pad = jnp.zeros((lora, w), F32).at[LORA_DECAY:].set(a_up)
    hid = jnp.arange(w) // HEAD_DIM
    bd = (hid[:, None] == hid[None, :]).astype(BF16)
    t = jnp.arange(CHUNK)
    ltri = (t[:, None] >= t[None, :]).astype(BF16)
    row = lambda vec: vec.reshape(1, -1)
    tok = lambda width: pl.BlockSpec((1, tm, width), lambda b, j: (b, j, 0))
    ins = [tok(cols),
           pl.BlockSpec((1, 8, cols), lambda b, j: (b, jnp.maximum(j * (tm // 8) - 1, 0), 0)),
           _resident((1, cols)), _resident((1, w)), _resident((lora, w)), _resident((1, w)),
           _resident((lora, w)), _resident((LORA_GATE, w)), _resident((1, w)), _resident((1, w)),
           _resident((1, w)), _resident((w, w)), _resident((CHUNK, CHUNK))]
    outs = [tok(w)] * 7 + [pl.BlockSpec((1, nck, w), lambda b, j: (b, j, 0)), tok(w), tok(w)]
    shapes = ([jax.ShapeDtypeStruct((bsz, seq, w), BF16)] * 7
              + [jax.ShapeDtypeStruct((bsz, seq // CHUNK, w), F32)]
              + [jax.ShapeDtypeStruct((bsz, seq, w), F32)] * 2)
    return pl.pallas_call(
        _rwkv_prep_kernel,
        grid=(bsz, seq // tm),
        in_specs=ins, out_specs=outs, out_shape=shapes,
        compiler_params=_params(("parallel", "parallel")),
        name="rwkv_prep",
    )(p3, p3, row(mu), row(w0), wup_pad, row(a0), aup_pad.astype(BF16), g_up.astype(BF16), row(k_k), row(k_a),
      row(r_k), bd, ltri)


def _head_tile(z, head_diag):
    z = z.astype(BF16)
    reps = head_diag.shape[0] // z.shape[0]
    return jnp.where(head_diag, jnp.concatenate([z] * reps, axis=0), jnp.zeros(head_diag.shape, BF16))


def _rwkv_scan_stages(rt_ref, at_ref, bt_ref, kt_ref, bh_ref, kh_ref, v_ref, gam_ref, g_ref, bonus_ref,
                      lnw_ref, lnb_ref, bd_ref, o_ref, s_scr, y_scr):
    @pl.when(pl.program_id(1) == 0)
    def _():
        s_scr[...] = jnp.zeros_like(s_scr)

    quad = QUAD_LANES
    r1 = lax.broadcasted_iota(jnp.int32, (quad, quad), 0)
    c1 = lax.broadcasted_iota(jnp.int32, (quad, quad), 1)
    head_diag = (r1 // CHUNK) == (c1 // HEAD_DIM)
    ts = lax.broadcasted_iota(jnp.int32, (CHUNK, quad), 0)
    ss = lax.broadcasted_iota(jnp.int32, (CHUNK, quad), 1) % HEAD_DIM
    strict = ss < ts
    incl = ss <= ts
    eye = (ss == ts).astype(F32)
    nquad = rt_ref.shape[2] // quad
    units = [(c, q) for c in range(SCAN_GROUP) for q in range(nquad)]
    tile = functools.partial(_head_tile, head_diag=head_diag)
    lane_cat = lambda x, y: jnp.concatenate([x, y], axis=1)
    lane_head = lax.broadcasted_iota(jnp.int32, (CHUNK, quad), 1) // HEAD_DIM

    def fold(t):
        out = t[:CHUNK]
        for h in range(1, quad // HEAD_DIM):
            out = jnp.where(lane_head == h, t[h * CHUNK:(h + 1) * CHUNK], out)
        return out

    row_cat = lambda x, y: jnp.concatenate([x, y], axis=0)

    def group_stages(gi):
        base = gi * (SCAN_GROUP * CHUNK)
        ops = []
        for c, q in units:
            rows = pl.ds(pl.multiple_of(base + c * CHUNK, CHUNK), CHUNK)
            lanes = slice(q * quad, (q + 1) * quad)
            ops.append([ref[0, rows, lanes] for ref in (rt_ref, at_ref, bt_ref, kt_ref, bh_ref, kh_ref, v_ref)])
        yield
        rt, at, bt, kt, bh, kh, v = (list(x) for x in zip(*ops))
        rng = range(len(units))
        gram = [_dot_nt(row_cat(at[i], rt[i]), row_cat(tile(bt[i]), tile(kt[i]))) for i in rng]
        yield
        pw = [jnp.where(strict, gram[i][:CHUNK, :quad], 0.0) for i in rng]
        akrk = [row_cat(jnp.where(strict, gram[i][:CHUNK, quad:], 0.0),
                        jnp.where(incl, gram[i][CHUNK:, quad:], 0.0)).astype(BF16) for i in rng]
        a_rb = [jnp.where(incl, gram[i][CHUNK:, :quad], 0.0).astype(BF16) for i in rng]
        yield
        t_inv = [eye + pw[i] for i in rng]
        yield
        pw = [_dot(pw[i], tile(pw[i])) for i in rng]
        for _ in range(4):
            yield
            both = [_dot(row_cat(t_inv[i], pw[i]), tile(pw[i])) for i in rng]
            t_inv = [t_inv[i] + both[i][:CHUNK] for i in rng]
            pw = [both[i][CHUNK:] for i in rng]
        yield
        t_inv = [t_inv[i] + _dot(t_inv[i], tile(pw[i])) for i in rng]
        yield
        avrk = [_dot(akrk[i], tile(v[i])) for i in rng]
        yield
        uw = [_dot(t_inv[i], lane_cat(tile(avrk[i][:CHUNK]), tile(at[i]))) for i in rng]
        u = [x[:, :quad] for x in uw]
        wm = [x[:, quad:] for x in uw]
        yield
        qy = [_dot(a_rb[i], lane_cat(tile(wm[i]), tile(u[i]))) for i in rng]
        qh = [rt[i].astype(F32) + qy[i][:, :quad] for i in rng]
        yi = [qy[i][:, quad:] + avrk[i][CHUNK:] for i in rng]
        yield
        mp = [jnp.where(head_diag, _dot_tn(wm[i], bh[i]), 0.0).astype(BF16) for i in rng]
        yield
        nps = [fold(_dot_tn(row_cat(u[i].astype(BF16), v[i]), row_cat(bh[i], kh[i]))) for i in rng]
        yield
        st = [s_scr[q] for q in range(nquad)]
        for i, (c, q) in enumerate(units):
            lanes = slice(q * quad, (q + 1) * quad)
            y_scr[pl.ds(pl.multiple_of(base + c * CHUNK, CHUNK), CHUNK), lanes] = _dot_nt(qh[i], tile(st[q])) + yi[i]
            gam = gam_ref[0, pl.ds(gi * SCAN_GROUP + c, 1), lanes]
            st[q] = st[q] * gam + _dot(st[q], mp[i]) + nps[i]
            yield
        for q in range(nquad):
            s_scr[q] = st[q]

    assert rt_ref.shape[1] == SCAN_GROUP * CHUNK
    yield from group_stages(0)
    yield

    y = y_scr[...]
    bd = bd_ref[...]
    inv_n = 1.0 / HEAD_DIM
    mean = _dot_f32_lhs(y, bd, terms=2) * inv_n
    d = y - mean
    var = _dot(d * d, bd) * inv_n
    yn = d * lax.rsqrt(var + RWKV_GN_EPS) * lnw_ref[...] + lnb_ref[...]
    o_ref[0] = ((yn + bonus_ref[0]) * g_ref[0]).astype(o_ref.dtype)
    yield


def _rwkv_scan_specs(prep, ln_w, ln_b):
    rt = prep[0]
    bsz, seq, w = rt.shape
    tm = TOKEN_TILE
    nck = tm // CHUNK
    hid = jnp.arange(w) // HEAD_DIM
    bd = (hid[:, None] == hid[None, :]).astype(BF16)
    tok = pl.BlockSpec((1, tm, w), lambda b, j: (b, j, 0))
    ins = [tok] * 7 + [pl.BlockSpec((1, nck, w), lambda b, j: (b, j, 0)), tok, tok,
                       _resident((1, w)), _resident((1, w)), _resident((w, w))]
    args = (*prep, ln_w.reshape(1, w), ln_b.reshape(1, w), bd)
    scratch = [pltpu.VMEM((w // QUAD_LANES, HEAD_DIM, QUAD_LANES), F32), pltpu.VMEM((tm, w), F32)]
    return args, ins, tok, jax.ShapeDtypeStruct((bsz, seq, w), BF16), scratch


ATTN_PLANE = 128
ATTN_UNROLL = 2


def _attn_stages(q_ref, kp_ref, kc_ref, vp_ref, vc_ref, bias_ref, o_ref, l_ref,
                 q_scr, k_scr, v_scr, o_scr, l_scr, *, dilation):
    blk = ATTN_BLK
    span = blk * dilation
    tile = q_ref.shape[1]
    nplane = q_ref.shape[2] // ATTN_PLANE
    first = pl.program_id(1) == 0
    for j in range(nplane):
        ls = slice(j * ATTN_PLANE, (j + 1) * ATTN_PLANE)
        q_scr[j] = q_ref[0, :, ls].astype(F32) * (HEAD_DIM ** -0.5)
        k_scr[j, 0:span] = kp_ref[0, :, ls].astype(F32)
        k_scr[j, span:] = kc_ref[0, :, ls].astype(F32)
        v_scr[j, 0:span] = vp_ref[0, :, ls].astype(F32)
        v_scr[j, span:] = vc_ref[0, :, ls].astype(F32)
    col = lax.broadcasted_iota(jnp.int32, (blk, 2 * blk), 1)
    lane = lax.broadcasted_iota(jnp.int32, (blk, ATTN_PLANE), 1)
    per_plane = ATTN_PLANE // HEAD_DIM
    mine = [(lane // HEAD_DIM) == hh for hh in range(per_plane)]
    heads = [(j, hh) for j in range(nplane) for hh in range(per_plane)]

    def rows_of(idx):
        start = (idx % dilation) + (idx // dilation) * span
        if dilation == 1:
            start = pl.multiple_of(start, blk)
            return pl.ds(start, blk), pl.ds(start, 2 * blk)
        return pl.ds(start, blk, stride=dilation), pl.ds(start, 2 * blk, stride=dilation)

    def pair_stages(it):
        blocks = [it * ATTN_UNROLL + b for b in range(ATTN_UNROLL)]
        rows = [rows_of(idx) for idx in blocks]
        exists = [jnp.logical_or(jnp.logical_or(jnp.logical_not(first), idx >= dilation), col >= blk)
                  for idx in blocks]
        qp = [[q_scr[j, qr, :].astype(BF16) for j in range(nplane)] for qr, _ in rows]
        kp = [[k_scr[j, kr, :].astype(BF16) for j in range(nplane)] for _, kr in rows]
        vp = [[v_scr[j, kr, :].astype(BF16) for j in range(nplane)] for _, kr in rows]
        yield
        streams = [(b, i) for b in range(ATTN_UNROLL) for i in range(len(heads))]
        s = [_dot_nt(jnp.where(mine[heads[i][1]], qp[b][heads[i][0]], jnp.zeros_like(qp[b][heads[i][0]])),
                     kp[b][heads[i][0]]) for b, i in streams]
        yield
        s = [jnp.where(exists[b], s[n] + bias_ref[i], MASK_VALUE) for n, (b, i) in enumerate(streams)]
        yield
        m = [jnp.max(x, axis=-1, keepdims=True) for x in s]
        yield
        pexp = [jnp.exp(x - mx) for x, mx in zip(s, m)]
        yield
        l = [jnp.sum(x, axis=-1, keepdims=True) for x in pexp]
        yield
        o = [_dot(pexp[n], vp[b][heads[i][0]]) / l[n] for n, (b, i) in enumerate(streams)]
        yield
        lse = [mx + jnp.log(lx) for mx, lx in zip(m, l)]
        yield
        for b in range(ATTN_UNROLL):
            for j in range(nplane):
                n0 = b * len(heads) + j * per_plane
                o_pair = o[n0]
                l_pair = jnp.broadcast_to(lse[n0], o_pair.shape)
                for hh in range(1, per_plane):
                    o_pair = jnp.where(mine[hh], o[n0 + hh], o_pair)
                    l_pair = jnp.where(mine[hh], lse[n0 + hh], l_pair)
                o_scr[j, rows[b][0], :] = o_pair
                l_scr[j, rows[b][0], :] = l_pair
        yield

    yield
    for it in range(tile // (blk * ATTN_UNROLL)):
        yield from pair_stages(it)
    for j in range(nplane):
        ls = slice(j * ATTN_PLANE, (j + 1) * ATTN_PLANE)
        o_ref[0, :, ls] = o_scr[j]
        l_ref[0, :, ls] = l_scr[j]
    yield


def _attn_kernel(*refs, dilation):
    for _ in _attn_stages(*refs, dilation=dilation):
        pass


SCAN_REFS = (13, 1, 2)
ATTN_REFS = (6, 2, 5)


def _scan_attn_kernel(*refs, dilations):
    counts = [SCAN_REFS] + [ATTN_REFS] * len(dilations)
    parts = [[] for _ in counts]
    pos = 0
    for kind in range(3):
        for part, cnt in zip(parts, counts):
            part.extend(refs[pos:pos + cnt[kind]])
            pos += cnt[kind]
    live = [_rwkv_scan_stages(*parts[0])]
    live += [_attn_stages(*part, dilation=dil) for part, dil in zip(parts[1:], dilations)]
    done = object()
    while live:
        live = [gen for gen in live if next(gen, done) is not done]


def _attn_bias(group, dilation):
    blk = ATTN_BLK
    heads = jnp.arange(group * HEADS_PER_GROUP + 1, (group + 1) * HEADS_PER_GROUP + 1, dtype=F32)
    slopes = jnp.exp2(-ALIBI_MAX_EXP * heads / ATTN_HEADS)
    steps = blk + jnp.arange(blk)[:, None] - jnp.arange(2 * blk)[None, :]
    in_window = (steps >= 0) & (steps <= blk)
    alibi = -slopes[:, None, None] * (steps * dilation).astype(F32)[None]
    return jnp.where(in_window[None], alibi, MASK_VALUE)


def _attn_specs(p_attn3, group, dilation):
    bsz, seq, cols = p_attn3.shape
    blk = ATTN_BLK
    gw = ATTN_OUT_WIDTH
    span = blk * dilation
    tile = max(span, TOKEN_TILE)
    nplane = gw // ATTN_PLANE
    per_kind = ATTN_WIDTH // gw

    def cur(kind):
        return pl.BlockSpec((1, tile, gw), lambda b, n: (b, n, kind * per_kind + group))

    def prev(kind):
        return pl.BlockSpec((1, span, gw),
                            lambda b, n: (b, jnp.maximum(n * (tile // span) - 1, 0), kind * per_kind + group))

    out_spec = pl.BlockSpec((1, tile, gw), lambda b, n: (b, n, 0))
    plane = lambda rows: pltpu.VMEM((nplane, rows, ATTN_PLANE), F32)
    args = (p_attn3, p_attn3, p_attn3, p_attn3, p_attn3, _attn_bias(group, dilation))
    ins = [cur(0), prev(1), cur(1), prev(2), cur(2), _resident((HEADS_PER_GROUP, blk, 2 * blk))]
    scratch = [plane(tile), plane(span + tile), plane(span + tile), plane(tile), plane(tile)]
    return args, ins, [out_spec, out_spec], [jax.ShapeDtypeStruct((bsz, seq, gw), F32)] * 2, scratch, tile


def _attn_group(p_attn3, group, dilation):
    args, ins, outs, shapes, scratch, tile = _attn_specs(p_attn3, group, dilation)
    bsz, seq, _ = p_attn3.shape
    return pl.pallas_call(
        functools.partial(_attn_kernel, dilation=dilation),
        grid=(bsz, seq // tile),
        in_specs=ins, out_specs=outs, out_shape=shapes, scratch_shapes=scratch,
        compiler_params=_params(("parallel", "arbitrary")),
        name=f"attn_g{group}",
    )(*args)


def _rwkv_scan_attn(prep, ln_w, ln_b, p_attn3, groups):
    s_args, s_ins, s_out, s_shape, s_scratch = _rwkv_scan_specs(prep, ln_w, ln_b)
    args, ins, outs, shapes, scratch = list(s_args), list(s_ins), [s_out], [s_shape], list(s_scratch)
    for group in groups:
        a_args, a_ins, a_outs, a_shapes, a_scratch, tile = _attn_specs(p_attn3, group, ATTN_GROUPS[group][1])
        assert tile == TOKEN_TILE
        args += a_args
        ins += a_ins
        outs += a_outs
        shapes += a_shapes
        scratch += a_scratch
    bsz, seq, _ = p_attn3.shape
    res = pl.pallas_call(
        functools.partial(_scan_attn_kernel, dilations=tuple(ATTN_GROUPS[g][1] for g in groups)),
        grid=(bsz, seq // TOKEN_TILE),
        in_specs=ins, out_specs=outs, out_shape=shapes, scratch_shapes=scratch,
        compiler_params=_params(("parallel", "arbitrary")),
        name="rwkv_scan_attn",
    )(*args)
    return res[0], [(res[1 + 2 * i], res[2 + 2 * i]) for i in range(len(groups))]


def _merge_kernel(x_ref, ya_ref, pc_ref, pcprev_ref, o0_ref, o1_ref, o2_ref, l0_ref, l1_ref, l2_ref,
                  gate_ref, convw_ref, wb_ref, wout_ref, gain_ref, out_ref):
    first = pl.program_id(1) == 0
    cw = CONV_WIDTH
    pc = pc_ref[0].astype(F32)
    pprev = pcprev_ref[0].astype(F32)
    u = pc[:, cw:2 * cw] * pc[:, 2 * cw:]
    uprev = jnp.where(first, 0.0, pprev[:, cw:2 * cw] * pprev[:, 2 * cw:])
    cwt = convw_ref[...]
    yb = pc[:, :cw] * (cwt[0:1] * _shift_rows(u, uprev, 2) + cwt[1:2] * _shift_rows(u, uprev, 1) + cwt[2:3] * u)

    l0, l1, l2 = l0_ref[0], l1_ref[0], l2_ref[0]
    mx = jnp.maximum(jnp.maximum(l0, l1), l2)
    e0, e1, e2 = jnp.exp(l0 - mx), jnp.exp(l1 - mx), jnp.exp(l2 - mx)
    yc = (e0 * o0_ref[0] + e1 * o1_ref[0] + e2 * o2_ref[0]) / (e0 + e1 + e2)

    d = x_ref.shape[-1]
    gate = gate_ref[0].astype(F32)
    ra, rb = RWKV_WIDTH, RWKV_WIDTH + CONV_WIDTH
    merged = (_sigmoid(gate[:, :d]) * _dot(ya_ref[0], wb_ref[:ra])
              + _sigmoid(gate[:, d:2 * d]) * _dot(yb, wb_ref[ra:rb])
              + _sigmoid(gate[:, 2 * d:]) * _dot(yc, wb_ref[rb:]))
    out_ref[0] = x_ref[0] + _rms(_dot(merged, wout_ref[...]), gain_ref[...])


def _merge(x3, ya, p_conv3, attn, p_gate3, conv_w, wb_bf16, wout_bf16, gain, layer):
    bsz, seq, d = x3.shape
    tm = TOKEN_TILE
    tok = lambda width: pl.BlockSpec((1, tm, width), lambda b, j: (b, j, 0))
    (o0, l0), (o1, l1), (o2, l2) = attn
    gw = ATTN_OUT_WIDTH
    ins = [tok(d), tok(RWKV_WIDTH), tok(CONV_COLS),
           pl.BlockSpec((1, 8, CONV_COLS), lambda b, j: (b, jnp.maximum(j * (tm // 8) - 1, 0), 0)),
           tok(gw), tok(gw), tok(gw), tok(gw), tok(gw), tok(gw), tok(p_gate3.shape[-1]),
           _resident(conv_w.shape), _layer_resident(wb_bf16.shape, layer), _layer_resident(wout_bf16.shape, layer),
           _resident((1, d))]
    return pl.pallas_call(
        _merge_kernel,
        grid=(bsz, seq // tm),
        in_specs=ins, out_specs=tok(d),
        out_shape=jax.ShapeDtypeStruct((bsz, seq, d), F32),
        compiler_params=_params(("parallel", "parallel")),
        name="merge",
    )(x3, ya, p_conv3, p_conv3, o0, o1, o2, l0, l1, l2, p_gate3, conv_w, wb_bf16, wout_bf16, gain.reshape(1, d))


def _ffn_kernel(x_ref, gpre_ref, gpost_ref, win_ref, wout_ref, out_ref, act_scr):
    x = x_ref[...]
    h = _rms(x, gpre_ref[...]).astype(BF16)
    dff = wout_ref.shape[0]
    for c in range(0, dff, MXU_TILE):
        gate = jnp.dot(h, win_ref[:, c:c + MXU_TILE], preferred_element_type=F32)
        up = jnp.dot(h, win_ref[:, dff + c:dff + c + MXU_TILE], preferred_element_type=F32)
        act_scr[:, c:c + MXU_TILE] = (gate * _sigmoid(gate) * up).astype(BF16)
    z = jnp.dot(act_scr[...], wout_ref[...], preferred_element_type=F32)
    out_ref[...] = x + _rms(z, gpost_ref[...])


def _ffn(x2d, gpre, gpost, win_bf16, wout_bf16, layer):
    m, d = x2d.shape
    tm = FFN_TILE
    dff = wout_bf16.shape[1]
    return pl.pallas_call(
        _ffn_kernel,
        grid=(m // tm,),
        in_specs=[pl.BlockSpec((tm, d), lambda i: (i, 0)), _resident((1, d)), _resident((1, d)),
                  _layer_resident(win_bf16.shape, layer), _layer_resident(wout_bf16.shape, layer)],
        out_specs=pl.BlockSpec((tm, d), lambda i: (i, 0)),
        out_shape=jax.ShapeDtypeStruct((m, d), F32),
        scratch_shapes=[pltpu.VMEM((tm, dff), BF16)],
        compiler_params=_params(("parallel",)),
        name="ffn",
    )(x2d, gpre.reshape(1, d), gpost.reshape(1, d), win_bf16, wout_bf16)


def _layer(layer, x3, big, norm_mix_pre, norm_mix_post, norm_ffn_pre, norm_ffn_post, rwkv_mu, rwkv_w0, rwkv_w_up,
           rwkv_a0, rwkv_a_up, rwkv_g_up, rwkv_k_k, rwkv_k_a, rwkv_r_k, rwkv_ln_w, rwkv_ln_b, conv_w):
    w_in, w_branch, w_out, w_ffn_in, w_ffn_out = big
    bsz, seq, d = x3.shape
    m = bsz * seq
    p_rwkv, p_conv, p_attn, p_gate = _inproj(x3.reshape(m, d), norm_mix_pre, w_in, layer)
    in3 = lambda t: t.reshape(bsz, seq, t.shape[-1])
    prep = _rwkv_prep(in3(p_rwkv), rwkv_mu, rwkv_w0, rwkv_w_up, rwkv_a0, rwkv_a_up, rwkv_g_up,
                      rwkv_k_k, rwkv_k_a, rwkv_r_k.reshape(-1))
    fused = [gi for gi, (_, dil) in enumerate(ATTN_GROUPS) if ATTN_BLK * dil <= TOKEN_TILE]
    ya, attn = _rwkv_scan_attn(prep, rwkv_ln_w, rwkv_ln_b, in3(p_attn), fused)
    attn += [_attn_group(in3(p_attn), gi, dil) for gi, (_, dil) in enumerate(ATTN_GROUPS) if gi not in fused]
    x3 = _merge(x3, ya, in3(p_conv), attn, in3(p_gate), conv_w, w_branch, w_out, norm_mix_post, layer)
    x2 = _ffn(x3.reshape(m, d), norm_ffn_pre, norm_ffn_post, w_ffn_in, w_ffn_out, layer)
    return x2.reshape(bsz, seq, d)


def kernel(x, norm_mix_pre, norm_mix_post, norm_ffn_pre, norm_ffn_post, w_in, rwkv_mu, rwkv_w0, rwkv_w_up, rwkv_a0, rwkv_a_up, rwkv_g_up, rwkv_k_k, rwkv_k_a, rwkv_r_k, rwkv_ln_w, rwkv_ln_b, conv_w, w_branch, w_out, w_ffn_in, w_ffn_out):
    big = tuple(w.astype(BF16) for w in (w_in, w_branch, w_out, w_ffn_in, w_ffn_out))
    small = (norm_mix_pre, norm_mix_post, norm_ffn_pre, norm_ffn_post, rwkv_mu, rwkv_w0, rwkv_w_up,
             rwkv_a0, rwkv_a_up, rwkv_g_up, rwkv_k_k, rwkv_k_a, rwkv_r_k, rwkv_ln_w, rwkv_ln_b, conv_w)
    for layer in range(w_in.shape[0]):
        x = _layer(layer, x, big, *(p[layer] for p in small))
    return x
```

```python
import functools
import math

import jax
import jax.numpy as jnp
from jax import lax
from jax.experimental import pallas as pl
from jax.experimental.pallas import tpu as pltpu

F32 = jnp.float32
BF16 = jnp.bfloat16

HEAD_DIM = 64
NORM_EPS = 1e-6
RWKV_HEADS = 8
RWKV_WIDTH = RWKV_HEADS * HEAD_DIM
LORA_DECAY = 64
LORA_AAA = 64
LORA_GATE = 128
RWKV_COLS = 3 * RWKV_WIDTH + LORA_DECAY + LORA_AAA + LORA_GATE
RWKV_GN_EPS = 64e-5
CONV_WIDTH = 512
CONV_COLS = 3 * CONV_WIDTH
ATTN_GROUPS = ((128, 1), (512, 4), (2048, 16))
HEADS_PER_GROUP = 4
ATTN_HEADS = HEADS_PER_GROUP * len(ATTN_GROUPS)
ATTN_WIDTH = ATTN_HEADS * HEAD_DIM
ATTN_COLS = 3 * ATTN_WIDTH
ATTN_OUT_WIDTH = HEADS_PER_GROUP * HEAD_DIM
ALIBI_MAX_EXP = 8.0
ATTN_BLK = 128
MASK_VALUE = -1e30
DECAY_SCALE = -math.exp(-0.5)

MXU_TILE = 256
CHUNK = 64
QUAD_LANES = MXU_TILE
SCAN_GROUP = 8
TOKEN_TILE = 512
FFN_TILE = 1024
VMEM_LIMIT = 56 * 1024 * 1024


def _params(sem):
    return pltpu.CompilerParams(dimension_semantics=sem, vmem_limit_bytes=VMEM_LIMIT)


def _resident(shape):
    zeros = (0,) * len(shape)
    return pl.BlockSpec(shape, lambda *_: zeros, pipeline_mode=pl.Buffered(1))


def _layer_resident(stacked_shape, layer):
    tail = tuple(stacked_shape[1:])
    zeros = (0,) * len(tail)
    return pl.BlockSpec((None,) + tail, lambda *_: (layer,) + zeros, pipeline_mode=pl.Buffered(1))


def _dot(a, b):
    return jnp.dot(a.astype(BF16), b.astype(BF16), preferred_element_type=F32)


def _dot_nt(a, b):
    return lax.dot_general(a.astype(BF16), b.astype(BF16), (((1,), (1,)), ((), ())), preferred_element_type=F32)


def _dot_tn(a, b):
    return lax.dot_general(a.astype(BF16), b.astype(BF16), (((0,), (0,)), ((), ())), preferred_element_type=F32)


def _split3(x):
    hi = x.astype(BF16)
    r1 = x - hi.astype(F32)
    mid = r1.astype(BF16)
    lo = (r1 - mid.astype(F32)).astype(BF16)
    return hi, mid, lo


def _dot_f32_lhs(a, b01, terms=3):
    return sum(jnp.dot(x, b01, preferred_element_type=F32) for x in _split3(a)[:terms])


def _dot_x3(a, b):
    ah = a.astype(BF16)
    al = (a - ah.astype(F32)).astype(BF16)
    bh = b.astype(BF16)
    bl = (b - bh.astype(F32)).astype(BF16)
    d = lambda x, y: jnp.dot(x, y, preferred_element_type=F32)
    return d(ah, bh) + d(ah, bl) + d(al, bh)


def _rms(x, gain):
    return x * lax.rsqrt(jnp.mean(x * x, axis=-1, keepdims=True) + NORM_EPS) * gain


def _sigmoid(z):
    return 0.5 * jnp.tanh(0.5 * z) + 0.5


def _shift_rows(cur, prev_tail, n):
    rows = lax.broadcasted_iota(jnp.int32, cur.shape, 0)
    out = pltpu.roll(cur, n, axis=0)
    for i in range(n):
        out = jnp.where(rows == i, prev_tail[8 - n + i:8 - n + i + 1, :], out)
    return out


def _inproj_kernel(x_ref, gain_ref, w_ref, o_rwkv, o_conv, o_attn, o_gate):
    h = _rms(x_ref[...], gain_ref[...]).astype(BF16)
    lo = 0
    for o_ref in (o_rwkv, o_conv, o_attn, o_gate):
        width = o_ref.shape[-1]
        for c in range(0, width, MXU_TILE):
            o_ref[:, c:c + MXU_TILE] = jnp.dot(h, w_ref[:, lo + c:lo + c + MXU_TILE],
                                               preferred_element_type=F32).astype(o_ref.dtype)
        lo += width


def _inproj(x2d, gain, w_in_bf16, layer):
    m, d = x2d.shape
    tm = TOKEN_TILE
    widths = (RWKV_COLS, CONV_COLS, ATTN_COLS, w_in_bf16.shape[2] - RWKV_COLS - CONV_COLS - ATTN_COLS)
    dtypes = (F32, BF16, BF16, BF16)
    return pl.pallas_call(
        _inproj_kernel,
        grid=(m // tm,),
        in_specs=[pl.BlockSpec((tm, d), lambda i: (i, 0)),
                  _resident((1, d)),
                  _layer_resident(w_in_bf16.shape, layer)],
        out_specs=[pl.BlockSpec((tm, w), lambda i: (i, 0)) for w in widths],
        out_shape=[jax.ShapeDtypeStruct((m, w), dt) for w, dt in zip(widths, dtypes)],
        compiler_params=_params(("parallel",)),
        name="inproj",
    )(x2d, gain.reshape(1, d), w_in_bf16)


def _rwkv_prep_kernel(p_ref, prev_ref, mu_ref, w0_ref, wup_ref, a0_ref, aup_ref, gup_ref, kk_ref, ka_ref,
                      rk_ref, bd_ref, ltri_ref,
                      rt_ref, at_ref, bt_ref, kt_ref, bh_ref, kh_ref, v_ref, gam_ref, g_ref, bonus_ref):
    first = pl.program_id(1) == 0
    p = p_ref[0]
    prev = jnp.where(first, 0.0, prev_ref[0])
    xm = p + (_shift_rows(p, prev, 1) - p) * mu_ref[...]
    w = RWKV_WIDTH
    r, k, v = xm[:, 0:w], xm[:, w:2 * w], xm[:, 2 * w:3 * w]
    wa = xm[:, 3 * w:3 * w + LORA_DECAY + LORA_AAA]
    gl = xm[:, 3 * w + LORA_DECAY + LORA_AAA:]
    lw = DECAY_SCALE * _sigmoid(w0_ref[...] + _dot_x3(jnp.tanh(wa), wup_ref[...]))
    a_lr = _sigmoid(a0_ref[...] + _dot(wa, aup_ref[...]))
    g = _dot(_sigmoid(gl), gup_ref[...])
    bd = bd_ref[...]
    kk = k * kk_ref[...]
    kk = kk * lax.rsqrt(jnp.maximum(_dot(kk * kk, bd), 1e-12))
    k2 = k * (1.0 + (a_lr - 1.0) * ka_ref[...])
    bonus = _dot(r * k2 * rk_ref[...], bd) * v
    a = -kk
    b = kk * a_lr
    ltri = ltri_ref[...]
    parts = _split3(lw)
    cums, tots = [], []
    for c in range(p.shape[0] // CHUNK):
        rc = slice(c * CHUNK, (c + 1) * CHUNK)
        cum_c = sum(jnp.dot(ltri, x[rc], preferred_element_type=F32) for x in parts)
        cums.append(cum_c)
        tots.append(cum_c[CHUNK - 1:CHUNK])
    cum = jnp.concatenate(cums, axis=0)
    tot = jnp.concatenate([jnp.broadcast_to(t, (CHUNK, t.shape[1])) for t in tots], axis=0)
    e_neg = jnp.exp(-cum)
    e_rem = jnp.exp(tot - cum)
    rt_ref[0] = (r * jnp.exp(cum)).astype(BF16)
    at_ref[0] = (a * jnp.exp(cum - lw)).astype(BF16)
    bt_ref[0] = (b * e_neg).astype(BF16)
    kt_ref[0] = (k2 * e_neg).astype(BF16)
    bh_ref[0] = (b * e_rem).astype(BF16)
    kh_ref[0] = (k2 * e_rem).astype(BF16)
    v_ref[0] = v.astype(BF16)
    gam_ref[0] = jnp.exp(jnp.concatenate(tots, axis=0))
    g_ref[0] = g
    bonus_ref[0] = bonus


def _rwkv_prep(p3, mu, w0, w_up, a0, a_up, g_up, k_k, k_a, r_k):
    bsz, seq, cols = p3.shape
    tm = TOKEN_TILE
    w = RWKV_WIDTH
    nck = tm // CHUNK
    lora = LORA_DECAY + LORA_AAA
    wup_pad = jnp.zeros((lora, w), F32).at[:LORA_DECAY].set(w_up)
    aup_pad = jnp.zeros((lora, w), F32).at[LORA_DECAY:].set(a_up)
    hid = jnp.arange(w) // HEAD_DIM
    bd = (hid[:, None] == hid[None, :]).astype(BF16)
    t = jnp.arange(CHUNK)
    ltri = (t[:, None] >= t[None, :]).astype(BF16)
    row = lambda vec: vec.reshape(1, -1)
    tok = lambda width: pl.BlockSpec((1, tm, width), lambda b, j: (b, j, 0))
    ins = [tok(cols),
           pl.BlockSpec((1, 8, cols), lambda b, j: (b, jnp.maximum(j * (tm // 8) - 1, 0), 0)),
           _resident((1, cols)), _resident((1, w)), _resident((lora, w)), _resident((1, w)),
           _resident((lora, w)), _resident((LORA_GATE, w)), _resident((1, w)), _resident((1, w)),
           _resident((1, w)), _resident((w, w)), _resident((CHUNK, CHUNK))]
    outs = [tok(w)] * 7 + [pl.BlockSpec((1, nck, w), lambda b, j: (b, j, 0)), tok(w), tok(w)]
    shapes = ([jax.ShapeDtypeStruct((bsz, seq, w), BF16)] * 7
              + [jax.ShapeDtypeStruct((bsz, seq // CHUNK, w), F32)]
              + [jax.ShapeDtypeStruct((bsz, seq, w), F32)] * 2)
    return pl.pallas_call(
        _rwkv_prep_kernel,
        grid=(bsz, seq // tm),
        in_specs=ins, out_specs=outs, out_shape=shapes,
        compiler_params=_params(("parallel", "parallel")),
        name="rwkv_prep",
    )(p3, p3, row(mu), row(w0), wup_pad, row(a0), aup_pad.astype(BF16), g_up.astype(BF16), row(k_k), row(k_a),
      row(r_k), bd, ltri)


def _head_tile(z, head_diag):
    z = z.astype(BF16)
    reps = head_diag.shape[0] // z.shape[0]
    return jnp.where(head_diag, jnp.concatenate([z] * reps, axis=0), jnp.zeros(head_diag.shape, BF16))


def _rwkv_scan_stages(rt_ref, at_ref, bt_ref, kt_ref, bh_ref, kh_ref, v_ref, gam_ref, g_ref, bonus_ref,
                      lnw_ref, lnb_ref, bd_ref, o_ref, s_scr, y_scr):
    @pl.when(pl.program_id(1) == 0)
    def _():
        s_scr[...] = jnp.zeros_like(s_scr)

    quad = QUAD_LANES
    r1 = lax.broadcasted_iota(jnp.int32, (quad, quad), 0)
    c1 = lax.broadcasted_iota(jnp.int32, (quad, quad), 1)
    head_diag = (r1 // CHUNK) == (c1 // HEAD_DIM)
    ts = lax.broadcasted_iota(jnp.int32, (CHUNK, quad), 0)
    ss = lax.broadcasted_iota(jnp.int32, (CHUNK, quad), 1) % HEAD_DIM
    strict = ss < ts
    incl = ss <= ts
    eye = (ss == ts).astype(F32)
    nquad = rt_ref.shape[2] // quad
    units = [(c, q) for c in range(SCAN_GROUP) for q in range(nquad)]
    tile = functools.partial(_head_tile, head_diag=head_diag)
    lane_cat = lambda x, y: jnp.concatenate([x, y], axis=1)
    lane_head = lax.broadcasted_iota(jnp.int32, (CHUNK, quad), 1) // HEAD_DIM

    def fold(t):
        out = t[:CHUNK]
        for h in range(1, quad // HEAD_DIM):
            out = jnp.where(lane_head == h, t[h * CHUNK:(h + 1) * CHUNK], out)
        return out

    row_cat = lambda x, y: jnp.concatenate([x, y], axis=0)

    def group_stages(gi):
        base = gi * (SCAN_GROUP * CHUNK)
        ops = []
        for c, q in units:
            rows = pl.ds(pl.multiple_of(base + c * CHUNK, CHUNK), CHUNK)
            lanes = slice(q * quad, (q + 1) * quad)
            ops.append([ref[0, rows, lanes] for ref in (rt_ref, at_ref, bt_ref, kt_ref, bh_ref, kh_ref, v_ref)])
        yield
        rt, at, bt, kt, bh, kh, v = (list(x) for x in zip(*ops))
        rng = range(len(units))
        gram = [_dot_nt(row_cat(at[i], rt[i]), row_cat(tile(bt[i]), tile(kt[i]))) for i in rng]
        yield
        pw = [jnp.where(strict, gram[i][:CHUNK, :quad], 0.0) for i in rng]
        akrk = [row_cat(jnp.where(strict, gram[i][:CHUNK, quad:], 0.0),
                        jnp.where(incl, gram[i][CHUNK:, quad:], 0.0)).astype(BF16) for i in rng]
        a_rb = [jnp.where(incl, gram[i][CHUNK:, :quad], 0.0).astype(BF16) for i in rng]
        yield
        t_inv = [eye + pw[i] for i in rng]
        yield
        pw = [_dot(pw[i], tile(pw[i])) for i in rng]
        for _ in range(4):
            yield
            both = [_dot(row_cat(t_inv[i], pw[i]), tile(pw[i])) for i in rng]
            t_inv = [t_inv[i] + both[i][:CHUNK] for i in rng]
            pw = [both[i][CHUNK:] for i in rng]
        yield
        t_inv = [t_inv[i] + _dot(t_inv[i], tile(pw[i])) for i in rng]
        yield
        avrk = [_dot(akrk[i], tile(v[i])) for i in rng]
        yield
        uw = [_dot(t_inv[i], lane_cat(tile(avrk[i][:CHUNK]), tile(at[i]))) for i in rng]
        u = [x[:, :quad] for x in uw]
        wm = [x[:, quad:] for x in uw]
        yield
        qy = [_dot(a_rb[i], lane_cat(tile(wm[i]), tile(u[i]))) for i in rng]
        qh = [rt[i].astype(F32) + qy[i][:, :quad] for i in rng]
        yi = [qy[i][:, quad:] + avrk[i][CHUNK:] for i in rng]
        yield
        mp = [jnp.where(head_diag, _dot_tn(wm[i], bh[i]), 0.0).astype(BF16) for i in rng]
        yield
        nps = [fold(_dot_tn(row_cat(u[i].astype(BF16), v[i]), row_cat(bh[i], kh[i]))) for i in rng]
        yield
        st = [s_scr[q] for q in range(nquad)]
        for i, (c, q) in enumerate(units):
            lanes = slice(q * quad, (q + 1) * quad)
            y_scr[pl.ds(pl.multiple_of(base + c * CHUNK, CHUNK), CHUNK), lanes] = _dot_nt(qh[i], tile(st[q])) + yi[i]
            gam = gam_ref[0, pl.ds(gi * SCAN_GROUP + c, 1), lanes]
            st[q] = st[q] * gam + _dot(st[q], mp[i]) + nps[i]
            yield
        for q in range(nquad):
            s_scr[q] = st[q]

    assert rt_ref.shape[1] == SCAN_GROUP * CHUNK
    yield from group_stages(0)
    yield

    y = y_scr[...]
    bd = bd_ref[...]
    inv_n = 1.0 / HEAD_DIM
    mean = _dot_f32_lhs(y, bd, terms=2) * inv_n
    d = y - mean
    var = _dot(d * d, bd) * inv_n
    yn = d * lax.rsqrt(var + RWKV_GN_EPS) * lnw_ref[...] + lnb_ref[...]
    o_ref[0] = ((yn + bonus_ref[0]) * g_ref[0]).astype(o_ref.dtype)
    yield


def _rwkv_scan_specs(prep, ln_w, ln_b):
    rt = prep[0]
    bsz, seq, w = rt.shape
    tm = TOKEN_TILE
    nck = tm // CHUNK
    hid = jnp.arange(w) // HEAD_DIM
    bd = (hid[:, None] == hid[None, :]).astype(BF16)
    tok = pl.BlockSpec((1, tm, w), lambda b, j: (b, j, 0))
    ins = [tok] * 7 + [pl.BlockSpec((1, nck, w), lambda b, j: (b, j, 0)), tok, tok,
                       _resident((1, w)), _resident((1, w)), _resident((w, w))]
    args = (*prep, ln_w.reshape(1, w), ln_b.reshape(1, w), bd)
    scratch = [pltpu.VMEM((w // QUAD_LANES, HEAD_DIM, QUAD_LANES), F32), pltpu.VMEM((tm, w), F32)]
    return args, ins, tok, jax.ShapeDtypeStruct((bsz, seq, w), BF16), scratch


ATTN_PLANE = 128
ATTN_UNROLL = 2
ATTN_MAX_STRIDE = 4


def _attn_stages(q_ref, kp_ref, kc_ref, vp_ref, vc_ref, bias_ref, o_ref, l_ref,
                 q_scr, k_scr, v_scr, o_scr, l_scr, *hop_scr, dilation):
    blk = ATTN_BLK
    span = blk * dilation
    tile = q_ref.shape[1]
    nplane = q_ref.shape[2] // ATTN_PLANE
    first = pl.program_id(1) == 0
    hop = max(dilation // ATTN_MAX_STRIDE, 1)
    inner = dilation // hop
    for j in range(nplane):
        ls = slice(j * ATTN_PLANE, (j + 1) * ATTN_PLANE)
        q_scr[j] = q_ref[0, :, ls].astype(F32) * (HEAD_DIM ** -0.5)
        k_scr[j, 0:span] = kp_ref[0, :, ls].astype(F32)
        k_scr[j, span:] = kc_ref[0, :, ls].astype(F32)
        v_scr[j, 0:span] = vp_ref[0, :, ls].astype(F32)
        v_scr[j, span:] = vc_ref[0, :, ls].astype(F32)
    if hop > 1:
        assert tile == span
        for j in range(nplane):
            for b in range(hop):
                for src, dst in ((q_scr, hop_scr[0]), (k_scr, hop_scr[1]), (v_scr, hop_scr[2])):
                    n = src.shape[1] // hop
                    dst[j, b * n:(b + 1) * n] = src[j, pl.ds(b, n, stride=hop), :]
    q_src, k_src, v_src, o_dst, l_dst = ((q_scr, k_scr, v_scr, o_scr, l_scr) if hop == 1 else hop_scr)
    col = lax.broadcasted_iota(jnp.int32, (blk, 2 * blk), 1)
    lane = lax.broadcasted_iota(jnp.int32, (blk, ATTN_PLANE), 1)
    per_plane = ATTN_PLANE // HEAD_DIM
    mine = [(lane // HEAD_DIM) == hh for hh in range(per_plane)]
    heads = [(j, hh) for j in range(nplane) for hh in range(per_plane)]

    def rows_of(idx):
        if hop > 1:
            a, b = (idx % dilation) // hop, idx % hop
            return (pl.ds(b * (tile // hop) + a, blk, stride=inner),
                    pl.ds(b * ((span + tile) // hop) + a, 2 * blk, stride=inner))
        start = (idx % dilation) + (idx // dilation) * span
        if dilation == 1:
            start = pl.multiple_of(start, blk)
            return pl.ds(start, blk), pl.ds(start, 2 * blk)
        return pl.ds(start, blk, stride=dilation), pl.ds(start, 2 * blk, stride=dilation)

    def pair_stages(it):
        blocks = [it * ATTN_UNROLL + b for b in range(ATTN_UNROLL)]
        rows = [rows_of(idx) for idx in blocks]
        exists = [jnp.logical_or(jnp.logical_or(jnp.logical_not(first), idx >= dilation), col >= blk)
                  for idx in blocks]
        qp = [[q_src[j, qr, :].astype(BF16) for j in range(nplane)] for qr, _ in rows]
        kp = [[k_src[j, kr, :].astype(BF16) for j in range(nplane)] for _, kr in rows]
        vp = [[v_src[j, kr, :].astype(BF16) for j in range(nplane)] for _, kr in rows]
        yield
        streams = [(b, i) for b in range(ATTN_UNROLL) for i in range(len(heads))]
        s = [_dot_nt(jnp.where(mine[heads[i][1]], qp[b][heads[i][0]], jnp.zeros_like(qp[b][heads[i][0]])),
                     kp[b][heads[i][0]]) for b, i in streams]
        yield
        s = [jnp.where(exists[b], s[n] + bias_ref[i], MASK_VALUE) for n, (b, i) in enumerate(streams)]
        yield
        m = [jnp.max(x, axis=-1, keepdims=True) for x in s]
        yield
        pexp = [jnp.exp(x - mx) for x, mx in zip(s, m)]
        yield
        l = [jnp.sum(x, axis=-1, keepdims=True) for x in pexp]
        yield
        o = [_dot(pexp[n], vp[b][heads[i][0]]) / l[n] for n, (b, i) in enumerate(streams)]
        yield
        lse = [mx + jnp.log(lx) for mx, lx in zip(m, l)]
        yield
        for b in range(ATTN_UNROLL):
            for j in range(nplane):
                n0 = b * len(heads) + j * per_plane
                o_pair = o[n0]
                l_pair = jnp.broadcast_to(lse[n0], o_pair.shape)
                for hh in range(1, per_plane):
                    o_pair = jnp.where(mine[hh], o[n0 + hh], o_pair)
                    l_pair = jnp.where(mine[hh], lse[n0 + hh], l_pair)
                o_dst[j, rows[b][0], :] = o_pair
                l_dst[j, rows[b][0], :] = l_pair
        yield

    yield
    for it in range(tile // (blk * ATTN_UNROLL)):
        yield from pair_stages(it)
    for j in range(nplane):
        if hop > 1:
            n = tile // hop
            for b in range(hop):
                o_scr[j, pl.ds(b, n, stride=hop), :] = o_dst[j, b * n:(b + 1) * n]
                l_scr[j, pl.ds(b, n, stride=hop), :] = l_dst[j, b * n:(b + 1) * n]
        ls = slice(j * ATTN_PLANE, (j + 1) * ATTN_PLANE)
        o_ref[0, :, ls] = o_scr[j]
        l_ref[0, :, ls] = l_scr[j]
    yield


def _attn_kernel(*refs, dilation):
    for _ in _attn_stages(*refs, dilation=dilation):
        pass


SCAN_REFS = (13, 1, 2)
ATTN_REFS = (6, 2, 5)


def _scan_attn_kernel(*refs, dilations):
    counts = [SCAN_REFS] + [ATTN_REFS] * len(dilations)
    parts = [[] for _ in counts]
    pos = 0
    for kind in range(3):
        for part, cnt in zip(parts, counts):
            part.extend(refs[pos:pos + cnt[kind]])
            pos += cnt[kind]
    live = [_rwkv_scan_stages(*parts[0])]
    live += [_attn_stages(*part, dilation=dil) for part, dil in zip(parts[1:], dilations)]
    done = object()
    while live:
        live = [gen for gen in live if next(gen, done) is not done]


def _attn_bias(group, dilation):
    blk = ATTN_BLK
    heads = jnp.arange(group * HEADS_PER_GROUP + 1, (group + 1) * HEADS_PER_GROUP + 1, dtype=F32)
    slopes = jnp.exp2(-ALIBI_MAX_EXP * heads / ATTN_HEADS)
    steps = blk + jnp.arange(blk)[:, None] - jnp.arange(2 * blk)[None, :]
    in_window = (steps >= 0) & (steps <= blk)
    alibi = -slopes[:, None, None] * (steps * dilation).astype(F32)[None]
    return jnp.where(in_window[None], alibi, MASK_VALUE)


def _attn_specs(p_attn3, group, dilation):
    bsz, seq, cols = p_attn3.shape
    blk = ATTN_BLK
    gw = ATTN_OUT_WIDTH
    span = blk * dilation
    tile = max(span, TOKEN_TILE)
    nplane = gw // ATTN_PLANE
    per_kind = ATTN_WIDTH // gw

    def cur(kind):
        return pl.BlockSpec((1, tile, gw), lambda b, n: (b, n, kind * per_kind + group))

    def prev(kind):
        return pl.BlockSpec((1, span, gw),
                            lambda b, n: (b, jnp.maximum(n * (tile // span) - 1, 0), kind * per_kind + group))

    out_spec = pl.BlockSpec((1, tile, gw), lambda b, n: (b, n, 0))
    plane = lambda rows: pltpu.VMEM((nplane, rows, ATTN_PLANE), F32)
    args = (p_attn3, p_attn3, p_attn3, p_attn3, p_attn3, _attn_bias(group, dilation))
    ins = [cur(0), prev(1), cur(1), prev(2), cur(2), _resident((HEADS_PER_GROUP, blk, 2 * blk))]
    scratch = [plane(tile), plane(span + tile), plane(span + tile), plane(tile), plane(tile)]
    if dilation > ATTN_MAX_STRIDE:
        scratch = scratch + scratch
    return args, ins, [out_spec, out_spec], [jax.ShapeDtypeStruct((bsz, seq, gw), F32)] * 2, scratch, tile


def _attn_group(p_attn3, group, dilation):
    args, ins, outs, shapes, scratch, tile = _attn_specs(p_attn3, group, dilation)
    bsz, seq, _ = p_attn3.shape
    return pl.pallas_call(
        functools.partial(_attn_kernel, dilation=dilation),
        grid=(bsz, seq // tile),
        in_specs=ins, out_specs=outs, out_shape=shapes, scratch_shapes=scratch,
        compiler_params=_params(("parallel", "arbitrary")),
        name=f"attn_g{group}",
    )(*args)


def _rwkv_scan_attn(prep, ln_w, ln_b, p_attn3, groups):
    s_args, s_ins, s_out, s_shape, s_scratch = _rwkv_scan_specs(prep, ln_w, ln_b)
    args, ins, outs, shapes, scratch = list(s_args), list(s_ins), [s_out], [s_shape], list(s_scratch)
    for group in groups:
        a_args, a_ins, a_outs, a_shapes, a_scratch, tile = _attn_specs(p_attn3, group, ATTN_GROUPS[group][1])
        assert tile == TOKEN_TILE
        args += a_args
        ins += a_ins
        outs += a_outs
        shapes += a_shapes
        scratch += a_scratch
    bsz, seq, _ = p_attn3.shape
    res = pl.pallas_call(
        functools.partial(_scan_attn_kernel, dilations=tuple(ATTN_GROUPS[g][1] for g in groups)),
        grid=(bsz, seq // TOKEN_TILE),
        in_specs=ins, out_specs=outs, out_shape=shapes, scratch_shapes=scratch,
        compiler_params=_params(("parallel", "arbitrary")),
        name="rwkv_scan_attn",
    )(*args)
    return res[0], [(res[1 + 2 * i], res[2 + 2 * i]) for i in range(len(groups))]


def _merge_kernel(x_ref, ya_ref, pc_ref, pcprev_ref, o0_ref, o1_ref, o2_ref, l0_ref, l1_ref, l2_ref,
                  gate_ref, convw_ref, wb_ref, wout_ref, gain_ref, out_ref):
    first = pl.program_id(1) == 0
    cw = CONV_WIDTH
    pc = pc_ref[0].astype(F32)
    pprev = pcprev_ref[0].astype(F32)
    u = pc[:, cw:2 * cw] * pc[:, 2 * cw:]
    uprev = jnp.where(first, 0.0, pprev[:, cw:2 * cw] * pprev[:, 2 * cw:])
    cwt = convw_ref[...]
    yb = pc[:, :cw] * (cwt[0:1] * _shift_rows(u, uprev, 2) + cwt[1:2] * _shift_rows(u, uprev, 1) + cwt[2:3] * u)

    l0, l1, l2 = l0_ref[0], l1_ref[0], l2_ref[0]
    mx = jnp.maximum(jnp.maximum(l0, l1), l2)
    e0, e1, e2 = jnp.exp(l0 - mx), jnp.exp(l1 - mx), jnp.exp(l2 - mx)
    yc = (e0 * o0_ref[0] + e1 * o1_ref[0] + e2 * o2_ref[0]) / (e0 + e1 + e2)

    d = x_ref.shape[-1]
    gate = gate_ref[0].astype(F32)
    ra, rb = RWKV_WIDTH, RWKV_WIDTH + CONV_WIDTH
    merged = (_sigmoid(gate[:, :d]) * _dot(ya_ref[0], wb_ref[:ra])
              + _sigmoid(gate[:, d:2 * d]) * _dot(yb, wb_ref[ra:rb])
              + _sigmoid(gate[:, 2 * d:]) * _dot(yc, wb_ref[rb:]))
    out_ref[0] = x_ref[0] + _rms(_dot(merged, wout_ref[...]), gain_ref[...])


def _merge(x3, ya, p_conv3, attn, p_gate3, conv_w, wb_bf16, wout_bf16, gain, layer):
    bsz, seq, d = x3.shape
    tm = TOKEN_TILE
    tok = lambda width: pl.BlockSpec((1, tm, width), lambda b, j: (b, j, 0))
    (o0, l0), (o1, l1), (o2, l2) = attn
    gw = ATTN_OUT_WIDTH
    ins = [tok(d), tok(RWKV_WIDTH), tok(CONV_COLS),
           pl.BlockSpec((1, 8, CONV_COLS), lambda b, j: (b, jnp.maximum(j * (tm // 8) - 1, 0), 0)),
           tok(gw), tok(gw), tok(gw), tok(gw), tok(gw), tok(gw), tok(p_gate3.shape[-1]),
           _resident(conv_w.shape), _layer_resident(wb_bf16.shape, layer), _layer_resident(wout_bf16.shape, layer),
           _resident((1, d))]
    return pl.pallas_call(
        _merge_kernel,
        grid=(bsz, seq // tm),
        in_specs=ins, out_specs=tok(d),
        out_shape=jax.ShapeDtypeStruct((bsz, seq, d), F32),
        compiler_params=_params(("parallel", "parallel")),
        name="merge",
    )(x3, ya, p_conv3, p_conv3, o0, o1, o2, l0, l1, l2, p_gate3, conv_w, wb_bf16, wout_bf16, gain.reshape(1, d))


def _ffn_kernel(x_ref, gpre_ref, gpost_ref, win_ref, wout_ref, out_ref, act_scr):
    x = x_ref[...]
    h = _rms(x, gpre_ref[...]).astype(BF16)
    dff = wout_ref.shape[0]
    for c in range(0, dff, MXU_TILE):
        gate = jnp.dot(h, win_ref[:, c:c + MXU_TILE], preferred_element_type=F32)
        up = jnp.dot(h, win_ref[:, dff + c:dff + c + MXU_TILE], preferred_element_type=F32)
        act_scr[:, c:c + MXU_TILE] = (gate * _sigmoid(gate) * up).astype(BF16)
    z = jnp.dot(act_scr[...], wout_ref[...], preferred_element_type=F32)
    out_ref[...] = x + _rms(z, gpost_ref[...])


def _ffn(x2d, gpre, gpost, win_bf16, wout_bf16, layer):
    m, d = x2d.shape
    tm = FFN_TILE
    dff = wout_bf16.shape[1]
    return pl.pallas_call(
        _ffn_kernel,
        grid=(m // tm,),
        in_specs=[pl.BlockSpec((tm, d), lambda i: (i, 0)), _resident((1, d)), _resident((1, d)),
                  _layer_resident(win_bf16.shape, layer), _layer_resident(wout_bf16.shape, layer)],
        out_specs=pl.BlockSpec((tm, d), lambda i: (i, 0)),
        out_shape=jax.ShapeDtypeStruct((m, d), F32),
        scratch_shapes=[pltpu.VMEM((tm, dff), BF16)],
        compiler_params=_params(("parallel",)),
        name="ffn",
    )(x2d, gpre.reshape(1, d), gpost.reshape(1, d), win_bf16, wout_bf16)


def _layer(layer, x3, big, norm_mix_pre, norm_mix_post, norm_ffn_pre, norm_ffn_post, rwkv_mu, rwkv_w0, rwkv_w_up,
           rwkv_a0, rwkv_a_up, rwkv_g_up, rwkv_k_k, rwkv_k_a, rwkv_r_k, rwkv_ln_w, rwkv_ln_b, conv_w):
    w_in, w_branch, w_out, w_ffn_in, w_ffn_out = big
    bsz, seq, d = x3.shape
    m = bsz * seq
    p_rwkv, p_conv, p_attn, p_gate = _inproj(x3.reshape(m, d), norm_mix_pre, w_in, layer)
    in3 = lambda t: t.reshape(bsz, seq, t.shape[-1])
    prep = _rwkv_prep(in3(p_rwkv), rwkv_mu, rwkv_w0, rwkv_w_up, rwkv_a0, rwkv_a_up, rwkv_g_up,
                      rwkv_k_k, rwkv_k_a, rwkv_r_k.reshape(-1))
    fused = [gi for gi, (_, dil) in enumerate(ATTN_GROUPS) if ATTN_BLK * dil <= TOKEN_TILE]
    ya, attn = _rwkv_scan_attn(prep, rwkv_ln_w, rwkv_ln_b, in3(p_attn), fused)
    attn += [_attn_group(in3(p_attn), gi, dil) for gi, (_, dil) in enumerate(ATTN_GROUPS) if gi not in fused]
    x3 = _merge(x3, ya, in3(p_conv), attn, in3(p_gate), conv_w, w_branch, w_out, norm_mix_post, layer)
    x2 = _ffn(x3.reshape(m, d), norm_ffn_pre, norm_ffn_post, w_ffn_in, w_ffn_out, layer)
    return x2.reshape(bsz, seq, d)


def kernel(x, norm_mix_pre, norm_mix_post, norm_ffn_pre, norm_ffn_post, w_in, rwkv_mu, rwkv_w0, rwkv_w_up, rwkv_a0, rwkv_a_up, rwkv_g_up, rwkv_k_k, rwkv_k_a, rwkv_r_k, rwkv_ln_w, rwkv_ln_b, conv_w, w_branch, w_out, w_ffn_in, w_ffn_out):
    big = tuple(w.astype(BF16) for w in (w_in, w_branch, w_out, w_ffn_in, w_ffn_out))
    small = (norm_mix_pre, norm_mix_post, norm_ffn_pre, norm_ffn_post, rwkv_mu, rwkv_w0, rwkv_w_up,
             rwkv_a0, rwkv_a_up, rwkv_g_up, rwkv_k_k, rwkv_k_a, rwkv_r_k, rwkv_ln_w, rwkv_ln_b, conv_w)
    for layer in range(w_in.shape[0]):
        x = _layer(layer, x, big, *(p[layer] for p in small))
    return x
```

```python
import functools
import math

import jax
import jax.numpy as jnp
from jax import lax
from jax.experimental import pallas as pl
from jax.experimental.pallas import tpu as pltpu

F32 = jnp.float32
BF16 = jnp.bfloat16

HEAD_DIM = 64
NORM_EPS = 1e-6
RWKV_HEADS = 8
RWKV_WIDTH = RWKV_HEADS * HEAD_DIM
LORA_DECAY = 64
LORA_AAA = 64
LORA_GATE = 128
RWKV_COLS = 3 * RWKV_WIDTH + LORA_DECAY + LORA_AAA + LORA_GATE
RWKV_GN_EPS = 64e-5
CONV_WIDTH = 512
CONV_COLS = 3 * CONV_WIDTH
ATTN_GROUPS = ((128, 1), (512, 4), (2048, 16))
HEADS_PER_GROUP = 4
ATTN_HEADS = HEADS_PER_GROUP * len(ATTN_GROUPS)
ATTN_WIDTH = ATTN_HEADS * HEAD_DIM
ATTN_COLS = 3 * ATTN_WIDTH
ATTN_OUT_WIDTH = HEADS_PER_GROUP * HEAD_DIM
ALIBI_MAX_EXP = 8.0
ATTN_BLK = 128
MASK_VALUE = -1e30
DECAY_SCALE = -math.exp(-0.5)

MXU_TILE = 256
CHUNK = 64
QUAD_LANES = MXU_TILE
SCAN_GROUP = 8
TOKEN_TILE = 512
FFN_TILE = 1024
VMEM_LIMIT = 56 * 1024 * 1024


def _params(sem):
    return pltpu.CompilerParams(dimension_semantics=sem, vmem_limit_bytes=VMEM_LIMIT)


def _resident(shape):
    zeros = (0,) * len(shape)
    return pl.BlockSpec(shape, lambda *_: zeros, pipeline_mode=pl.Buffered(1))


def _layer_resident(stacked_shape, layer):
    tail = tuple(stacked_shape[1:])
    zeros = (0,) * len(tail)
    return pl.BlockSpec((None,) + tail, lambda *_: (layer,) + zeros, pipeline_mode=pl.Buffered(1))


def _dot(a, b):
    return jnp.dot(a.astype(BF16), b.astype(BF16), preferred_element_type=F32)


def _dot_nt(a, b):
    return lax.dot_general(a.astype(BF16), b.astype(BF16), (((1,), (1,)), ((), ())), preferred_element_type=F32)


def _dot_tn(a, b):
    return lax.dot_general(a.astype(BF16), b.astype(BF16), (((0,), (0,)), ((), ())), preferred_element_type=F32)


def _split3(x):
    hi = x.astype(BF16)
    r1 = x - hi.astype(F32)
    mid = r1.astype(BF16)
    lo = (r1 - mid.astype(F32)).astype(BF16)
    return hi, mid, lo


def _dot_f32_lhs(a, b01, terms=3):
    return sum(jnp.dot(x, b01, preferred_element_type=F32) for x in _split3(a)[:terms])


def _dot_x3(a, b):
    ah = a.astype(BF16)
    al = (a - ah.astype(F32)).astype(BF16)
    bh = b.astype(BF16)
    bl = (b - bh.astype(F32)).astype(BF16)
    d = lambda x, y: jnp.dot(x, y, preferred_element_type=F32)
    return d(ah, bh) + d(ah, bl) + d(al, bh)


def _rms(x, gain):
    return x * lax.rsqrt(jnp.mean(x * x, axis=-1, keepdims=True) + NORM_EPS) * gain


def _sigmoid(z):
    return 0.5 * jnp.tanh(0.5 * z) + 0.5


def _shift_rows(cur, prev_tail, n):
    rows = lax.broadcasted_iota(jnp.int32, cur.shape, 0)
    out = pltpu.roll(cur, n, axis=0)
    for i in range(n):
        out = jnp.where(rows == i, prev_tail[8 - n + i:8 - n + i + 1, :], out)
    return out


def _inproj_kernel(x_ref, gain_ref, w_ref, o_rwkv, o_conv, o_attn, o_gate):
    h = _rms(x_ref[...], gain_ref[...]).astype(BF16)
    lo = 0
    for o_ref in (o_rwkv, o_conv, o_attn, o_gate):
        width = o_ref.shape[-1]
        for c in range(0, width, MXU_TILE):
            o_ref[:, c:c + MXU_TILE] = jnp.dot(h, w_ref[:, lo + c:lo + c + MXU_TILE],
                                               preferred_element_type=F32).astype(o_ref.dtype)
        lo += width


def _inproj(x2d, gain, w_in_bf16, layer):
    m, d = x2d.shape
    tm = TOKEN_TILE
    widths = (RWKV_COLS, CONV_COLS, ATTN_COLS, w_in_bf16.shape[2] - RWKV_COLS - CONV_COLS - ATTN_COLS)
    dtypes = (F32, BF16, BF16, BF16)
    return pl.pallas_call(
        _inproj_kernel,
        grid=(m // tm,),
        in_specs=[pl.BlockSpec((tm, d), lambda i: (i, 0)),
                  _resident((1, d)),
                  _layer_resident(w_in_bf16.shape, layer)],
        out_specs=[pl.BlockSpec((tm, w), lambda i: (i, 0)) for w in widths],
        out_shape=[jax.ShapeDtypeStruct((m, w), dt) for w, dt in zip(widths, dtypes)],
        compiler_params=_params(("parallel",)),
        name="inproj",
    )(x2d, gain.reshape(1, d), w_in_bf16)


def _rwkv_prep_kernel(p_ref, prev_ref, mu_ref, w0_ref, wup_ref, a0_ref, aup_ref, gup_ref, kk_ref, ka_ref,
                      rk_ref, bd_ref, ltri_ref,
                      rt_ref, at_ref, bt_ref, kt_ref, bh_ref, kh_ref, v_ref, gam_ref, g_ref, bonus_ref):
    first = pl.program_id(1) == 0
    p = p_ref[0]
    prev = jnp.where(first, 0.0, prev_ref[0])
    xm = p + (_shift_rows(p, prev, 1) - p) * mu_ref[...]
    w = RWKV_WIDTH
    r, k, v = xm[:, 0:w], xm[:, w:2 * w], xm[:, 2 * w:3 * w]
    wa = xm[:, 3 * w:3 * w + LORA_DECAY + LORA_AAA]
    gl = xm[:, 3 * w + LORA_DECAY + LORA_AAA:]
    lw = DECAY_SCALE * _sigmoid(w0_ref[...] + _dot_x3(jnp.tanh(wa), wup_ref[...]))
    a_lr = _sigmoid(a0_ref[...] + _dot(wa, aup_ref[...]))
    g = _dot(_sigmoid(gl), gup_ref[...])
    bd = bd_ref[...]
    kk = k * kk_ref[...]
    kk = kk * lax.rsqrt(jnp.maximum(_dot(kk * kk, bd), 1e-12))
    k2 = k * (1.0 + (a_lr - 1.0) * ka_ref[...])
    bonus = _dot(r * k2 * rk_ref[...], bd) * v
    a = -kk
    b = kk * a_lr
    ltri = ltri_ref[...]
    parts = _split3(lw)
    cums, tots = [], []
    for c in range(p.shape[0] // CHUNK):
        rc = slice(c * CHUNK, (c + 1) * CHUNK)
        cum_c = sum(jnp.dot(ltri, x[rc], preferred_element_type=F32) for x in parts)
        cums.append(cum_c)
        tots.append(cum_c[CHUNK - 1:CHUNK])
    cum = jnp.concatenate(cums, axis=0)
    gam = jnp.exp(jnp.concatenate(tots, axis=0))
    e_neg = jnp.exp(-cum)
    e_rem = e_neg * jnp.concatenate([jnp.broadcast_to(gam[c:c + 1], (CHUNK, gam.shape[1])) for c in range(len(tots))],
                                    axis=0)
    rt_ref[0] = (r * jnp.exp(cum)).astype(BF16)
    at_ref[0] = (a * jnp.exp(cum - lw)).astype(BF16)
    bt_ref[0] = (b * e_neg).astype(BF16)
    kt_ref[0] = (k2 * e_neg).astype(BF16)
    bh_ref[0] = (b * e_rem).astype(BF16)
    kh_ref[0] = (k2 * e_rem).astype(BF16)
    v_ref[0] = v.astype(BF16)
    gam_ref[0] = gam
    g_ref[0] = g
    bonus_ref[0] = bonus


def _rwkv_prep(p3, mu, w0, w_up, a0, a_up, g_up, k_k, k_a, r_k):
    bsz, seq, cols = p3.shape
    tm = TOKEN_TILE
    w = RWKV_WIDTH
    nck = tm // CHUNK
    lora = LORA_DECAY + LORA_AAA
    wup_pad = jnp.zeros((lora, w), F32).at[:LORA_DECAY].set(w_up)
    aup_pad = jnp.zeros((lora, w), F32).at[LORA_DECAY:].set(a_up)
    hid = jnp.arange(w) // HEAD_DIM
    bd = (hid[:, None] == hid[None, :]).astype(BF16)
    t = jnp.arange(CHUNK)
    ltri = (t[:, None] >= t[None, :]).astype(BF16)
    row = lambda vec: vec.reshape(1, -1)
    tok = lambda width: pl.BlockSpec((1, tm, width), lambda b, j: (b, j, 0))
    ins = [tok(cols),
           pl.BlockSpec((1, 8, cols), lambda b, j: (b, jnp.maximum(j * (tm // 8) - 1, 0), 0)),
           _resident((1, cols)), _resident((1, w)), _resident((lora, w)), _resident((1, w)),
           _resident((lora, w)), _resident((LORA_GATE, w)), _resident((1, w)), _resident((1, w)),
           _resident((1, w)), _resident((w, w)), _resident((CHUNK, CHUNK))]
    outs = [tok(w)] * 7 + [pl.BlockSpec((1, nck, w), lambda b, j: (b, j, 0)), tok(w), tok(w)]
    shapes = ([jax.ShapeDtypeStruct((bsz, seq, w), BF16)] * 7
              + [jax.ShapeDtypeStruct((bsz, seq // CHUNK, w), F32)]
              + [jax.ShapeDtypeStruct((bsz, seq, w), F32)] * 2)
    return pl.pallas_call(
        _rwkv_prep_kernel,
        grid=(bsz, seq // tm),
        in_specs=ins, out_specs=outs, out_shape=shapes,
        compiler_params=_params(("parallel", "parallel")),
        name="rwkv_prep",
    )(p3, p3, row(mu), row(w0), wup_pad, row(a0), aup_pad.astype(BF16), g_up.astype(BF16), row(k_k), row(k_a),
      row(r_k), bd, ltri)


def _head_tile(z, head_diag):
    z = z.astype(BF16)
    reps = head_diag.shape[0] // z.shape[0]
    return jnp.where(head_diag, jnp.concatenate([z] * reps, axis=0), jnp.zeros(head_diag.shape, BF16))


def _rwkv_scan_stages(rt_ref, at_ref, bt_ref, kt_ref, bh_ref, kh_ref, v_ref, gam_ref, g_ref, bonus_ref,
                      lnw_ref, lnb_ref, bd_ref, o_ref, s_scr, y_scr):
    @pl.when(pl.program_id(1) == 0)
    def _():
        s_scr[...] = jnp.zeros_like(s_scr)

    quad = QUAD_LANES
    r1 = lax.broadcasted_iota(jnp.int32, (quad, quad), 0)
    c1 = lax.broadcasted_iota(jnp.int32, (quad, quad), 1)
    head_diag = (r1 // CHUNK) == (c1 // HEAD_DIM)
    ts = lax.broadcasted_iota(jnp.int32, (CHUNK, quad), 0)
    ss = lax.broadcasted_iota(jnp.int32, (CHUNK, quad), 1) % HEAD_DIM
    strict = ss < ts
    incl = ss <= ts
    eye = (ss == ts).astype(F32)
    nquad = rt_ref.shape[2] // quad
    units = [(c, q) for c in range(SCAN_GROUP) for q in range(nquad)]
    tile = functools.partial(_head_tile, head_diag=head_diag)
    lane_cat = lambda x, y: jnp.concatenate([x, y], axis=1)
    lane_head = lax.broadcasted_iota(jnp.int32, (CHUNK, quad), 1) // HEAD_DIM

    def fold(t):
        out = t[:CHUNK]
        for h in range(1, quad // HEAD_DIM):
            out = jnp.where(lane_head == h, t[h * CHUNK:(h + 1) * CHUNK], out)
        return out

    row_cat = lambda x, y: jnp.concatenate([x, y], axis=0)

    def group_stages(gi):
        base = gi * (SCAN_GROUP * CHUNK)
        ops = []
        for c, q in units:
            rows = pl.ds(pl.multiple_of(base + c * CHUNK, CHUNK), CHUNK)
            lanes = slice(q * quad, (q + 1) * quad)
            ops.append([ref[0, rows, lanes] for ref in (rt_ref, at_ref, bt_ref, kt_ref, bh_ref, kh_ref, v_ref)])
        yield
        rt, at, bt, kt, bh, kh, v = (list(x) for x in zip(*ops))
        rng = range(len(units))
        gram = [_dot_nt(row_cat(at[i], rt[i]), row_cat(tile(bt[i]), tile(kt[i]))) for i in rng]
        yield
        pw = [jnp.where(strict, gram[i][:CHUNK, :quad], 0.0) for i in rng]
        akrk = [row_cat(jnp.where(strict, gram[i][:CHUNK, quad:], 0.0),
                        jnp.where(incl, gram[i][CHUNK:, quad:], 0.0)).astype(BF16) for i in rng]
        a_rb = [jnp.where(incl, gram[i][CHUNK:, :quad], 0.0).astype(BF16) for i in rng]
        yield
        t_inv = [eye + pw[i] for i in rng]
        yield
        pw = [_dot(pw[i], tile(pw[i])) for i in rng]
        for _ in range(4):
            yield
            both = [_dot(row_cat(t_inv[i], pw[i]), tile(pw[i])) for i in rng]
            t_inv = [t_inv[i] + both[i][:CHUNK] for i in rng]
            pw = [both[i][CHUNK:] for i in rng]
        yield
        t_inv = [t_inv[i] + _dot(t_inv[i], tile(pw[i])) for i in rng]
        yield
        avrk = [_dot(akrk[i], tile(v[i])) for i in rng]
        yield
        uw = [_dot(t_inv[i], lane_cat(tile(avrk[i][:CHUNK]), tile(at[i]))) for i in rng]
        u = [x[:, :quad] for x in uw]
        wm = [x[:, quad:] for x in uw]
        yield
        qy = [_dot(a_rb[i], lane_cat(tile(wm[i]), tile(u[i]))) for i in rng]
        qh = [rt[i].astype(F32) + qy[i][:, :quad] for i in rng]
        yi = [qy[i][:, quad:] + avrk[i][CHUNK:] for i in rng]
        yield
        mp = [jnp.where(head_diag, _dot_tn(wm[i], bh[i]), 0.0).astype(BF16) for i in rng]
        yield
        nps = [fold(_dot_tn(row_cat(u[i].astype(BF16), v[i]), row_cat(bh[i], kh[i]))) for i in rng]
        yield
        st = [s_scr[q] for q in range(nquad)]
        for i, (c, q) in enumerate(units):
            lanes = slice(q * quad, (q + 1) * quad)
            y_scr[pl.ds(pl.multiple_of(base + c * CHUNK, CHUNK), CHUNK), lanes] = _dot_nt(qh[i], tile(st[q])) + yi[i]
            gam = gam_ref[0, pl.ds(gi * SCAN_GROUP + c, 1), lanes]
            st[q] = st[q] * gam + _dot(st[q], mp[i]) + nps[i]
            yield
        for q in range(nquad):
            s_scr[q] = st[q]

    assert rt_ref.shape[1] == SCAN_GROUP * CHUNK
    yield from group_stages(0)
    yield

    y = y_scr[...]
    bd = bd_ref[...]
    inv_n = 1.0 / HEAD_DIM
    mean = _dot(y, bd) * inv_n
    d = y - mean
    var = _dot(d * d, bd) * inv_n
    yn = d * lax.rsqrt(var + RWKV_GN_EPS) * lnw_ref[...] + lnb_ref[...]
    o_ref[0] = ((yn + bonus_ref[0]) * g_ref[0]).astype(o_ref.dtype)
    yield


def _rwkv_scan_specs(prep, ln_w, ln_b):
    rt = prep[0]
    bsz, seq, w = rt.shape
    tm = TOKEN_TILE
    nck = tm // CHUNK
    hid = jnp.arange(w) // HEAD_DIM
    bd = (hid[:, None] == hid[None, :]).astype(BF16)
    tok = pl.BlockSpec((1, tm, w), lambda b, j: (b, j, 0))
    ins = [tok] * 7 + [pl.BlockSpec((1, nck, w), lambda b, j: (b, j, 0)), tok, tok,
                       _resident((1, w)), _resident((1, w)), _resident((w, w))]
    args = (*prep, ln_w.reshape(1, w), ln_b.reshape(1, w), bd)
    scratch = [pltpu.VMEM((w // QUAD_LANES, HEAD_DIM, QUAD_LANES), F32), pltpu.VMEM((tm, w), F32)]
    return args, ins, tok, jax.ShapeDtypeStruct((bsz, seq, w), BF16), scratch


ATTN_PLANE = 128
ATTN_UNROLL = 2
ATTN_MAX_STRIDE = 4


def _attn_stages(q_ref, kp_ref, kc_ref, vp_ref, vc_ref, bias_ref, o_ref, l_ref,
                 q_scr, k_scr, v_scr, o_scr, l_scr, *hop_scr, dilation):
    blk = ATTN_BLK
    span = blk * dilation
    tile = q_ref.shape[1]
    nplane = q_ref.shape[2] // ATTN_PLANE
    first = pl.program_id(1) == 0
    hop = max(dilation // ATTN_MAX_STRIDE, 1)
    inner = dilation // hop
    for j in range(nplane):
        ls = slice(j * ATTN_PLANE, (j + 1) * ATTN_PLANE)
        q_scr[j] = q_ref[0, :, ls].astype(F32) * (HEAD_DIM ** -0.5)
        k_scr[j, 0:span] = kp_ref[0, :, ls].astype(F32)
        k_scr[j, span:] = kc_ref[0, :, ls].astype(F32)
        v_scr[j, 0:span] = vp_ref[0, :, ls].astype(F32)
        v_scr[j, span:] = vc_ref[0, :, ls].astype(F32)
    if hop > 1:
        assert tile == span
        for j in range(nplane):
            for b in range(hop):
                for src, dst in ((q_scr, hop_scr[0]), (k_scr, hop_scr[1]), (v_scr, hop_scr[2])):
                    n = src.shape[1] // hop
                    dst[j, b * n:(b + 1) * n] = src[j, pl.ds(b, n, stride=hop), :]
    q_src, k_src, v_src, o_dst, l_dst = ((q_scr, k_scr, v_scr, o_scr, l_scr) if hop == 1 else hop_scr)
    col = lax.broadcasted_iota(jnp.int32, (blk, 2 * blk), 1)
    lane = lax.broadcasted_iota(jnp.int32, (blk, ATTN_PLANE), 1)
    per_plane = ATTN_PLANE // HEAD_DIM
    mine = [(lane // HEAD_DIM) == hh for hh in range(per_plane)]
    heads = [(j, hh) for j in range(nplane) for hh in range(per_plane)]

    def rows_of(idx):
        if hop > 1:
            a, b = (idx % dilation) // hop, idx % hop
            return (pl.ds(b * (tile // hop) + a, blk, stride=inner),
                    pl.ds(b * ((span + tile) // hop) + a, 2 * blk, stride=inner))
        start = (idx % dilation) + (idx // dilation) * span
        if dilation == 1:
            start = pl.multiple_of(start, blk)
            return pl.ds(start, blk), pl.ds(start, 2 * blk)
        return pl.ds(start, blk, stride=dilation), pl.ds(start, 2 * blk, stride=dilation)

    def pair_stages(it):
        blocks = [it * ATTN_UNROLL + b for b in range(ATTN_UNROLL)]
        rows = [rows_of(idx) for idx in blocks]
        exists = [None if idx >= dilation else jnp.logical_or(jnp.logical_not(first), col >= blk)
                  for idx in blocks]
        qp = [[q_src[j, qr, :].astype(BF16) for j in range(nplane)] for qr, _ in rows]
        kp = [[k_src[j, kr, :].astype(BF16) for j in range(nplane)] for _, kr in rows]
        vp = [[v_src[j, kr, :].astype(BF16) for j in range(nplane)] for _, kr in rows]
        yield
        streams = [(b, i) for b in range(ATTN_UNROLL) for i in range(len(heads))]
        s = [_dot_nt(jnp.where(mine[heads[i][1]], qp[b][heads[i][0]], jnp.zeros_like(qp[b][heads[i][0]])),
                     kp[b][heads[i][0]]) for b, i in streams]
        yield
        s = [s[n] + bias_ref[i] for n, (b, i) in enumerate(streams)]
        s = [x if exists[b] is None else jnp.where(exists[b], x, MASK_VALUE) for x, (b, i) in zip(s, streams)]
        yield
        m = [jnp.max(x, axis=-1, keepdims=True) for x in s]
        yield
        pexp = [jnp.exp(x - mx) for x, mx in zip(s, m)]
        yield
        l = [jnp.sum(x, axis=-1, keepdims=True) for x in pexp]
        yield
        o = [_dot(pexp[n], vp[b][heads[i][0]]) / l[n] for n, (b, i) in enumerate(streams)]
        yield
        lse = [mx + jnp.log(lx) for mx, lx in zip(m, l)]
        yield
        for b in range(ATTN_UNROLL):
            for j in range(nplane):
                n0 = b * len(heads) + j * per_plane
                o_pair = o[n0]
                l_pair = jnp.broadcast_to(lse[n0], o_pair.shape)
                for hh in range(1, per_plane):
                    o_pair = jnp.where(mine[hh], o[n0 + hh], o_pair)
                    l_pair = jnp.where(mine[hh], lse[n0 + hh], l_pair)
                o_dst[j, rows[b][0], :] = o_pair
                l_dst[j, rows[b][0], :] = l_pair
        yield

    yield
    for it in range(tile // (blk * ATTN_UNROLL)):
        yield from pair_stages(it)
    for j in range(nplane):
        if hop > 1:
            n = tile // hop
            for b in range(hop):
                o_scr[j, pl.ds(b, n, stride=hop), :] = o_dst[j, b * n:(b + 1) * n]
                l_scr[j, pl.ds(b, n, stride=hop), :] = l_dst[j, b * n:(b + 1) * n]
        ls = slice(j * ATTN_PLANE, (j + 1) * ATTN_PLANE)
        o_ref[0, :, ls] = o_scr[j]
        l_ref[0, :, ls] = l_scr[j]
    yield


def _attn_kernel(*refs, dilation):
    for _ in _attn_stages(*refs, dilation=dilation):
        pass


SCAN_REFS = (13, 1, 2)
ATTN_REFS = (6, 2, 5)


def _scan_attn_kernel(*refs, dilations):
    counts = [SCAN_REFS] + [ATTN_REFS] * len(dilations)
    parts = [[] for _ in counts]
    pos = 0
    for kind in range(3):
        for part, cnt in zip(parts, counts):
            part.extend(refs[pos:pos + cnt[kind]])
            pos += cnt[kind]
    live = [_rwkv_scan_stages(*parts[0])]
    live += [_attn_stages(*part, dilation=dil) for part, dil in zip(parts[1:], dilations)]
    done = object()
    while live:
        live = [gen for gen in live if next(gen, done) is not done]


def _attn_bias(group, dilation):
    blk = ATTN_BLK
    heads = jnp.arange(group * HEADS_PER_GROUP + 1, (group + 1) * HEADS_PER_GROUP + 1, dtype=F32)
    slopes = jnp.exp2(-ALIBI_MAX_EXP * heads / ATTN_HEADS)
    steps = blk + jnp.arange(blk)[:, None] - jnp.arange(2 * blk)[None, :]
    in_window = (steps >= 0) & (steps <= blk)
    alibi = -slopes[:, None, None] * (steps * dilation).astype(F32)[None]
    return jnp.where(in_window[None], alibi, MASK_VALUE)


def _attn_specs(p_attn3, group, dilation):
    bsz, seq, cols = p_attn3.shape
    blk = ATTN_BLK
    gw = ATTN_OUT_WIDTH
    span = blk * dilation
    tile = max(span, TOKEN_TILE)
    nplane = gw // ATTN_PLANE
    per_kind = ATTN_WIDTH // gw

    def cur(kind):
        return pl.BlockSpec((1, tile, gw), lambda b, n: (b, n, kind * per_kind + group))

    def prev(kind):
        return pl.BlockSpec((1, span, gw),
                            lambda b, n: (b, jnp.maximum(n * (tile // span) - 1, 0), kind * per_kind + group))

    out_spec = pl.BlockSpec((1, tile, gw), lambda b, n: (b, n, 0))
    plane = lambda rows: pltpu.VMEM((nplane, rows, ATTN_PLANE), F32)
    args = (p_attn3, p_attn3, p_attn3, p_attn3, p_attn3, _attn_bias(group, dilation))
    ins = [cur(0), prev(1), cur(1), prev(2), cur(2), _resident((HEADS_PER_GROUP, blk, 2 * blk))]
    scratch = [plane(tile), plane(span + tile), plane(span + tile), plane(tile), plane(tile)]
    if dilation > ATTN_MAX_STRIDE:
        scratch = scratch + scratch
    return args, ins, [out_spec, out_spec], [jax.ShapeDtypeStruct((bsz, seq, gw), F32)] * 2, scratch, tile


def _attn_group(p_attn3, group, dilation):
    args, ins, outs, shapes, scratch, tile = _attn_specs(p_attn3, group, dilation)
    bsz, seq, _ = p_attn3.shape
    return pl.pallas_call(
        functools.partial(_attn_kernel, dilation=dilation),
        grid=(bsz, seq // tile),
        in_specs=ins, out_specs=outs, out_shape=shapes, scratch_shapes=scratch,
        compiler_params=_params(("parallel", "arbitrary")),
        name=f"attn_g{group}",
    )(*args)


def _rwkv_scan_attn(prep, ln_w, ln_b, p_attn3, groups):
    s_args, s_ins, s_out, s_shape, s_scratch = _rwkv_scan_specs(prep, ln_w, ln_b)
    args, ins, outs, shapes, scratch = list(s_args), list(s_ins), [s_out], [s_shape], list(s_scratch)
    for group in groups:
        a_args, a_ins, a_outs, a_shapes, a_scratch, tile = _attn_specs(p_attn3, group, ATTN_GROUPS[group][1])
        assert tile == TOKEN_TILE
        args += a_args
        ins += a_ins
        outs += a_outs
        shapes += a_shapes
        scratch += a_scratch
    bsz, seq, _ = p_attn3.shape
    res = pl.pallas_call(
        functools.partial(_scan_attn_kernel, dilations=tuple(ATTN_GROUPS[g][1] for g in groups)),
        grid=(bsz, seq // TOKEN_TILE),
        in_specs=ins, out_specs=outs, out_shape=shapes, scratch_shapes=scratch,
        compiler_params=_params(("parallel", "arbitrary")),
        name="rwkv_scan_attn",
    )(*args)
    return res[0], [(res[1 + 2 * i], res[2 + 2 * i]) for i in range(len(groups))]


def _merge_kernel(x_ref, ya_ref, pc_ref, pcprev_ref, o0_ref, o1_ref, o2_ref, l0_ref, l1_ref, l2_ref,
                  gate_ref, convw_ref, wb_ref, wout_ref, gain_ref, out_ref):
    first = pl.program_id(1) == 0
    cw = CONV_WIDTH
    pc = pc_ref[0].astype(F32)
    pprev = pcprev_ref[0].astype(F32)
    u = pc[:, cw:2 * cw] * pc[:, 2 * cw:]
    uprev = jnp.where(first, 0.0, pprev[:, cw:2 * cw] * pprev[:, 2 * cw:])
    cwt = convw_ref[...]
    yb = pc[:, :cw] * (cwt[0:1] * _shift_rows(u, uprev, 2) + cwt[1:2] * _shift_rows(u, uprev, 1) + cwt[2:3] * u)

    l0, l1, l2 = l0_ref[0], l1_ref[0], l2_ref[0]
    mx = jnp.maximum(jnp.maximum(l0, l1), l2)
    e0, e1, e2 = jnp.exp(l0 - mx), jnp.exp(l1 - mx), jnp.exp(l2 - mx)
    yc = (e0 * o0_ref[0] + e1 * o1_ref[0] + e2 * o2_ref[0]) / (e0 + e1 + e2)

    d = x_ref.shape[-1]
    gate = gate_ref[0].astype(F32)
    ra, rb = RWKV_WIDTH, RWKV_WIDTH + CONV_WIDTH
    merged = (_sigmoid(gate[:, :d]) * _dot(ya_ref[0], wb_ref[:ra])
              + _sigmoid(gate[:, d:2 * d]) * _dot(yb, wb_ref[ra:rb])
              + _sigmoid(gate[:, 2 * d:]) * _dot(yc, wb_ref[rb:]))
    out_ref[0] = x_ref[0] + _rms(_dot(merged, wout_ref[...]), gain_ref[...])


def _merge(x3, ya, p_conv3, attn, p_gate3, conv_w, wb_bf16, wout_bf16, gain, layer):
    bsz, seq, d = x3.shape
    tm = TOKEN_TILE
    tok = lambda width: pl.BlockSpec((1, tm, width), lambda b, j: (b, j, 0))
    (o0, l0), (o1, l1), (o2, l2) = attn
    gw = ATTN_OUT_WIDTH
    ins = [tok(d), tok(RWKV_WIDTH), tok(CONV_COLS),
           pl.BlockSpec((1, 8, CONV_COLS), lambda b, j: (b, jnp.maximum(j * (tm // 8) - 1, 0), 0)),
           tok(gw), tok(gw), tok(gw), tok(gw), tok(gw), tok(gw), tok(p_gate3.shape[-1]),
           _resident(conv_w.shape), _layer_resident(wb_bf16.shape, layer), _layer_resident(wout_bf16.shape, layer),
           _resident((1, d))]
    return pl.pallas_call(
        _merge_kernel,
        grid=(bsz, seq // tm),
        in_specs=ins, out_specs=tok(d),
        out_shape=jax.ShapeDtypeStruct((bsz, seq, d), F32),
        compiler_params=_params(("parallel", "parallel")),
        name="merge",
    )(x3, ya, p_conv3, p_conv3, o0, o1, o2, l0, l1, l2, p_gate3, conv_w, wb_bf16, wout_bf16, gain.reshape(1, d))


def _ffn_kernel(x_ref, gpre_ref, gpost_ref, win_ref, wout_ref, out_ref, act_scr):
    x = x_ref[...]
    h = _rms(x, gpre_ref[...]).astype(BF16)
    dff = wout_ref.shape[0]
    for c in range(0, dff, MXU_TILE):
        gate = jnp.dot(h, win_ref[:, c:c + MXU_TILE], preferred_element_type=F32)
        up = jnp.dot(h, win_ref[:, dff + c:dff + c + MXU_TILE], preferred_element_type=F32)
        act_scr[:, c:c + MXU_TILE] = (gate * _sigmoid(gate) * up).astype(BF16)
    z = jnp.dot(act_scr[...], wout_ref[...], preferred_element_type=F32)
    out_ref[...] = x + _rms(z, gpost_ref[...])


def _ffn(x2d, gpre, gpost, win_bf16, wout_bf16, layer):
    m, d = x2d.shape
    tm = FFN_TILE
    dff = wout_bf16.shape[1]
    return pl.pallas_call(
        _ffn_kernel,
        grid=(m // tm,),
        in_specs=[pl.BlockSpec((tm, d), lambda i: (i, 0)), _resident((1, d)), _resident((1, d)),
                  _layer_resident(win_bf16.shape, layer), _layer_resident(wout_bf16.shape, layer)],
        out_specs=pl.BlockSpec((tm, d), lambda i: (i, 0)),
        out_shape=jax.ShapeDtypeStruct((m, d), F32),
        scratch_shapes=[pltpu.VMEM((tm, dff), BF16)],
        compiler_params=_params(("parallel",)),
        name="ffn",
    )(x2d, gpre.reshape(1, d), gpost.reshape(1, d), win_bf16, wout_bf16)


def _layer(layer, x3, big, norm_mix_pre, norm_mix_post, norm_ffn_pre, norm_ffn_post, rwkv_mu, rwkv_w0, rwkv_w_up,
           rwkv_a0, rwkv_a_up, rwkv_g_up, rwkv_k_k, rwkv_k_a, rwkv_r_k, rwkv_ln_w, rwkv_ln_b, conv_w):
    w_in, w_branch, w_out, w_ffn_in, w_ffn_out = big
    bsz, seq, d = x3.shape
    m = bsz * seq
    p_rwkv, p_conv, p_attn, p_gate = _inproj(x3.reshape(m, d), norm_mix_pre, w_in, layer)
    in3 = lambda t: t.reshape(bsz, seq, t.shape[-1])
    prep = _rwkv_prep(in3(p_rwkv), rwkv_mu, rwkv_w0, rwkv_w_up, rwkv_a0, rwkv_a_up, rwkv_g_up,
                      rwkv_k_k, rwkv_k_a, rwkv_r_k.reshape(-1))
    fused = [gi for gi, (_, dil) in enumerate(ATTN_GROUPS) if ATTN_BLK * dil <= TOKEN_TILE]
    ya, attn = _rwkv_scan_attn(prep, rwkv_ln_w, rwkv_ln_b, in3(p_attn), fused)
    attn += [_attn_group(in3(p_attn), gi, dil) for gi, (_, dil) in enumerate(ATTN_GROUPS) if gi not in fused]
    x3 = _merge(x3, ya, in3(p_conv), attn, in3(p_gate), conv_w, w_branch, w_out, norm_mix_post, layer)
    x2 = _ffn(x3.reshape(m, d), norm_ffn_pre, norm_ffn_post, w_ffn_in, w_ffn_out, layer)
    return x2.reshape(bsz, seq, d)


def kernel(x, norm_mix_pre, norm_mix_post, norm_ffn_pre, norm_ffn_post, w_in, rwkv_mu, rwkv_w0, rwkv_w_up, rwkv_a0, rwkv_a_up, rwkv_g_up, rwkv_k_k, rwkv_k_a, rwkv_r_k, rwkv_ln_w, rwkv_ln_b, conv_w, w_branch, w_out, w_ffn_in, w_ffn_out):
    big = tuple(w.astype(BF16) for w in (w_in, w_branch, w_out, w_ffn_in, w_ffn_out))
    small = (norm_mix_pre, norm_mix_post, norm_ffn_pre, norm_ffn_post, rwkv_mu, rwkv_w0, rwkv_w_up,
             rwkv_a0, rwkv_a_up, rwkv_g_up, rwkv_k_k, rwkv_k_a, rwkv_r_k, rwkv_ln_w, rwkv_ln_b, conv_w)
    for layer in range(w_in.shape[0]):
        x = _layer(layer, x, big, *(p[layer] for p in small))
    return x
```

```python
import functools
import math

import jax
import jax.numpy as jnp
from jax import lax
from jax.experimental import pallas as pl
from jax.experimental.pallas import tpu as pltpu

F32 = jnp.float32
BF16 = jnp.bfloat16

HEAD_DIM = 64
NORM_EPS = 1e-6
RWKV_HEADS = 8
RWKV_WIDTH = RWKV_HEADS * HEAD_DIM
LORA_DECAY = 64
LORA_AAA = 64
LORA_GATE = 128
RWKV_COLS = 3 * RWKV_WIDTH + LORA_DECAY + LORA_AAA + LORA_GATE
RWKV_GN_EPS = 64e-5
CONV_WIDTH = 512
CONV_COLS = 3 * CONV_WIDTH
ATTN_GROUPS = ((128, 1), (512, 4), (2048, 16))
HEADS_PER_GROUP = 4
ATTN_HEADS = HEADS_PER_GROUP * len(ATTN_GROUPS)
ATTN_WIDTH = ATTN_HEADS * HEAD_DIM
ATTN_COLS = 3 * ATTN_WIDTH
ATTN_OUT_WIDTH = HEADS_PER_GROUP * HEAD_DIM
ALIBI_MAX_EXP = 8.0
ATTN_BLK = 128
MASK_VALUE = -1e30
DECAY_SCALE = -math.exp(-0.5)

MXU_TILE = 256
CHUNK = 64
QUAD_LANES = MXU_TILE
SCAN_GROUP = 8
TOKEN_TILE = 512
FFN_TILE = 1024
VMEM_LIMIT = 56 * 1024 * 1024


def _params(sem):
    return pltpu.CompilerParams(dimension_semantics=sem, vmem_limit_bytes=VMEM_LIMIT)


def _resident(shape):
    zeros = (0,) * len(shape)
    return pl.BlockSpec(shape, lambda *_: zeros, pipeline_mode=pl.Buffered(1))


def _layer_resident(stacked_shape, layer):
    tail = tuple(stacked_shape[1:])
    zeros = (0,) * len(tail)
    return pl.BlockSpec((None,) + tail, lambda *_: (layer,) + zeros, pipeline_mode=pl.Buffered(1))


def _dot(a, b):
    return jnp.dot(a.astype(BF16), b.astype(BF16), preferred_element_type=F32)


def _dot_nt(a, b):
    return lax.dot_general(a.astype(BF16), b.astype(BF16), (((1,), (1,)), ((), ())), preferred_element_type=F32)


def _dot_tn(a, b):
    return lax.dot_general(a.astype(BF16), b.astype(BF16), (((0,), (0,)), ((), ())), preferred_element_type=F32)


def _split3(x):
    hi = x.astype(BF16)
    r1 = x - hi.astype(F32)
    mid = r1.astype(BF16)
    lo = (r1 - mid.astype(F32)).astype(BF16)
    return hi, mid, lo


def _dot_f32_lhs(a, b01, terms=3):
    return sum(jnp.dot(x, b01, preferred_element_type=F32) for x in _split3(a)[:terms])


def _dot_x3(a, b):
    ah = a.astype(BF16)
    al = (a - ah.astype(F32)).astype(BF16)
    bh = b.astype(BF16)
    bl = (b - bh.astype(F32)).astype(BF16)
    d = lambda x, y: jnp.dot(x, y, preferred_element_type=F32)
    return d(ah, bh) + d(ah, bl) + d(al, bh)


def _rms(x, gain):
    return x * lax.rsqrt(jnp.mean(x * x, axis=-1, keepdims=True) + NORM_EPS) * gain


def _sigmoid(z):
    return 0.5 * jnp.tanh(0.5 * z) + 0.5


def _shift_rows(cur, prev_tail, n):
    rows = lax.broadcasted_iota(jnp.int32, cur.shape, 0)
    out = pltpu.roll(cur, n, axis=0)
    for i in range(n):
        out = jnp.where(rows == i, prev_tail[8 - n + i:8 - n + i + 1, :], out)
    return out


def _inproj_kernel(x_ref, gain_ref, w_ref, o_rwkv, o_conv, o_attn, o_gate):
    h = _rms(x_ref[...], gain_ref[...]).astype(BF16)
    lo = 0
    for o_ref in (o_rwkv, o_conv, o_attn, o_gate):
        width = o_ref.shape[-1]
        for c in range(0, width, MXU_TILE):
            o_ref[:, c:c + MXU_TILE] = jnp.dot(h, w_ref[:, lo + c:lo + c + MXU_TILE],
                                               preferred_element_type=F32).astype(o_ref.dtype)
        lo += width


def _inproj(x2d, gain, w_in_bf16, layer):
    m, d = x2d.shape
    tm = TOKEN_TILE
    widths = (RWKV_COLS, CONV_COLS, ATTN_COLS, w_in_bf16.shape[2] - RWKV_COLS - CONV_COLS - ATTN_COLS)
    dtypes = (F32, BF16, BF16, BF16)
    return pl.pallas_call(
        _inproj_kernel,
        grid=(m // tm,),
        in_specs=[pl.BlockSpec((tm, d), lambda i: (i, 0)),
                  _resident((1, d)),
                  _layer_resident(w_in_bf16.shape, layer)],
        out_specs=[pl.BlockSpec((tm, w), lambda i: (i, 0)) for w in widths],
        out_shape=[jax.ShapeDtypeStruct((m, w), dt) for w, dt in zip(widths, dtypes)],
        compiler_params=_params(("parallel",)),
        name="inproj",
    )(x2d, gain.reshape(1, d), w_in_bf16)


def _rwkv_prep_kernel(p_ref, prev_ref, mu_ref, w0_ref, wup_ref, a0_ref, aup_ref, gup_ref, kk_ref, ka_ref,
                      rk_ref, bd_ref, ltri_ref,
                      rt_ref, at_ref, bt_ref, kt_ref, bh_ref, kh_ref, v_ref, gam_ref, g_ref, bonus_ref):
    first = pl.program_id(1) == 0
    p = p_ref[0]
    prev = jnp.where(first, 0.0, prev_ref[0])
    xm = p + (_shift_rows(p, prev, 1) - p) * mu_ref[...]
    w = RWKV_WIDTH
    r, k, v = xm[:, 0:w], xm[:, w:2 * w], xm[:, 2 * w:3 * w]
    wa = xm[:, 3 * w:3 * w + LORA_DECAY + LORA_AAA]
    gl = xm[:, 3 * w + LORA_DECAY + LORA_AAA:]
    lw = DECAY_SCALE * _sigmoid(w0_ref[...] + _dot_x3(jnp.tanh(wa), wup_ref[...]))
    a_lr = _sigmoid(a0_ref[...] + _dot(wa, aup_ref[...]))
    g = _dot(_sigmoid(gl), gup_ref[...])
    bd = bd_ref[...]
    kk = k * kk_ref[...]
    kk = kk * lax.rsqrt(jnp.maximum(_dot(kk * kk, bd), 1e-12))
    k2 = k * (1.0 + (a_lr - 1.0) * ka_ref[...])
    bonus = _dot(r * k2 * rk_ref[...], bd) * v
    a = -kk
    b = kk * a_lr
    ltri = ltri_ref[...]
    parts = _split3(lw)
    cums, tots = [], []
    for c in range(p.shape[0] // CHUNK):
        rc = slice(c * CHUNK, (c + 1) * CHUNK)
        cum_c = sum(jnp.dot(ltri, x[rc], preferred_element_type=F32) for x in parts)
        cums.append(cum_c)
        tots.append(cum_c[CHUNK - 1:CHUNK])
    cum = jnp.concatenate(cums, axis=0)
    gam = jnp.exp(jnp.concatenate(tots, axis=0))
    e_neg = jnp.exp(-cum)
    e_rem = e_neg * jnp.concatenate([jnp.broadcast_to(gam[c:c + 1], (CHUNK, gam.shape[1])) for c in range(len(tots))],
                                    axis=0)
    rt_ref[0] = (r * jnp.exp(cum)).astype(BF16)
    at_ref[0] = (a * jnp.exp(cum - lw)).astype(BF16)
    bt_ref[0] = (b * e_neg).astype(BF16)
    kt_ref[0] = (k2 * e_neg).astype(BF16)
    bh_ref[0] = (b * e_rem).astype(BF16)
    kh_ref[0] = (k2 * e_rem).astype(BF16)
    v_ref[0] = v.astype(BF16)
    gam_ref[0] = gam
    g_ref[0] = g
    bonus_ref[0] = bonus


def _rwkv_prep(p3, mu, w0, w_up, a0, a_up, g_up, k_k, k_a, r_k):
    bsz, seq, cols = p3.shape
    tm = TOKEN_TILE
    w = RWKV_WIDTH
    nck = tm // CHUNK
    lora = LORA_DECAY + LORA_AAA
    wup_pad = jnp.zeros((lora, w), F32).at[:LORA_DECAY].set(w_up)
    aup_pad = jnp.zeros((lora, w), F32).at[LORA_DECAY:].set(a_up)
    hid = jnp.arange(w) // HEAD_DIM
    bd = (hid[:, None] == hid[None, :]).astype(BF16)
    t = jnp.arange(CHUNK)
    ltri = (t[:, None] >= t[None, :]).astype(BF16)
    row = lambda vec: vec.reshape(1, -1)
    tok = lambda width: pl.BlockSpec((1, tm, width), lambda b, j: (b, j, 0))
    ins = [tok(cols),
           pl.BlockSpec((1, 8, cols), lambda b, j: (b, jnp.maximum(j * (tm // 8) - 1, 0), 0)),
           _resident((1, cols)), _resident((1, w)), _resident((lora, w)), _resident((1, w)),
           _resident((lora, w)), _resident((LORA_GATE, w)), _resident((1, w)), _resident((1, w)),
           _resident((1, w)), _resident((w, w)), _resident((CHUNK, CHUNK))]
    outs = [tok(w)] * 7 + [pl.BlockSpec((1, nck, w), lambda b, j: (b, j, 0)), tok(w), tok(w)]
    shapes = ([jax.ShapeDtypeStruct((bsz, seq, w), BF16)] * 7
              + [jax.ShapeDtypeStruct((bsz, seq // CHUNK, w), F32)]
              + [jax.ShapeDtypeStruct((bsz, seq, w), F32)] * 2)
    return pl.pallas_call(
        _rwkv_prep_kernel,
        grid=(bsz, seq // tm),
        in_specs=ins, out_specs=outs, out_shape=shapes,
        compiler_params=_params(("parallel", "parallel")),
        name="rwkv_prep",
    )(p3, p3, row(mu), row(w0), wup_pad, row(a0), aup_pad.astype(BF16), g_up.astype(BF16), row(k_k), row(k_a),
      row(r_k), bd, ltri)


def _head_tile(z, head_diag):
    z = z.astype(BF16)
    reps = head_diag.shape[0] // z.shape[0]
    return jnp.where(head_diag, jnp.concatenate([z] * reps, axis=0), jnp.zeros(head_diag.shape, BF16))


def _rwkv_scan_stages(rt_ref, at_ref, bt_ref, kt_ref, bh_ref, kh_ref, v_ref, gam_ref, g_ref, bonus_ref,
                      lnw_ref, lnb_ref, bd_ref, o_ref, s_scr, y_scr):
    @pl.when(pl.program_id(1) == 0)
    def _():
        s_scr[...] = jnp.zeros_like(s_scr)

    quad = QUAD_LANES
    r1 = lax.broadcasted_iota(jnp.int32, (quad, quad), 0)
    c1 = lax.broadcasted_iota(jnp.int32, (quad, quad), 1)
    head_diag = (r1 // CHUNK) == (c1 // HEAD_DIM)
    ts = lax.broadcasted_iota(jnp.int32, (CHUNK, quad), 0)
    ss = lax.broadcasted_iota(jnp.int32, (CHUNK, quad), 1) % HEAD_DIM
    strict = ss < ts
    incl = ss <= ts
    eye = (ss == ts).astype(F32)
    nquad = rt_ref.shape[2] // quad
    units = [(c, q) for c in range(SCAN_GROUP) for q in range(nquad)]
    tile = functools.partial(_head_tile, head_diag=head_diag)
    lane_cat = lambda x, y: jnp.concatenate([x, y], axis=1)
    lane_head = lax.broadcasted_iota(jnp.int32, (CHUNK, quad), 1) // HEAD_DIM

    def fold(t):
        out = t[:CHUNK]
        for h in range(1, quad // HEAD_DIM):
            out = jnp.where(lane_head == h, t[h * CHUNK:(h + 1) * CHUNK], out)
        return out

    row_cat = lambda x, y: jnp.concatenate([x, y], axis=0)

    def group_stages(gi):
        base = gi * (SCAN_GROUP * CHUNK)
        ops = []
        for c, q in units:
            rows = pl.ds(pl.multiple_of(base + c * CHUNK, CHUNK), CHUNK)
            lanes = slice(q * quad, (q + 1) * quad)
            ops.append([ref[0, rows, lanes] for ref in (rt_ref, at_ref, bt_ref, kt_ref, bh_ref, kh_ref, v_ref)])
        yield
        rt, at, bt, kt, bh, kh, v = (list(x) for x in zip(*ops))
        rng = range(len(units))
        gram = [_dot_nt(row_cat(at[i], rt[i]), row_cat(tile(bt[i]), tile(kt[i]))) for i in rng]
        yield
        pw = [jnp.where(strict, gram[i][:CHUNK, :quad], 0.0) for i in rng]
        akrk = [row_cat(jnp.where(strict, gram[i][:CHUNK, quad:], 0.0),
                        jnp.where(incl, gram[i][CHUNK:, quad:], 0.0)).astype(BF16) for i in rng]
        a_rb = [jnp.where(incl, gram[i][CHUNK:, :quad], 0.0).astype(BF16) for i in rng]
        yield
        t_inv = [eye + pw[i] for i in rng]
        yield
        pw = [_dot(pw[i], tile(pw[i])) for i in rng]
        for _ in range(4):
            yield
            both = [_dot(row_cat(t_inv[i], pw[i]), tile(pw[i])) for i in rng]
            t_inv = [t_inv[i] + both[i][:CHUNK] for i in rng]
            pw = [both[i][CHUNK:] for i in rng]
        yield
        t_inv = [t_inv[i] + _dot(t_inv[i], tile(pw[i])) for i in rng]
        yield
        avrk = [_dot(akrk[i], tile(v[i])) for i in rng]
        yield
        uw = [_dot(t_inv[i], lane_cat(tile(avrk[i][:CHUNK]), tile(at[i]))) for i in rng]
        u = [x[:, :quad] for x in uw]
        wm = [x[:, quad:] for x in uw]
        yield
        qy = [_dot(a_rb[i], lane_cat(tile(wm[i]), tile(u[i]))) for i in rng]
        qh = [rt[i].astype(F32) + qy[i][:, :quad] for i in rng]
        yi = [qy[i][:, quad:] + avrk[i][CHUNK:] for i in rng]
        yield
        mp = [jnp.where(head_diag, _dot_tn(wm[i], bh[i]), 0.0).astype(BF16) for i in rng]
        yield
        nps = [fold(_dot_tn(row_cat(u[i].astype(BF16), v[i]), row_cat(bh[i], kh[i]))) for i in rng]
        yield
        st = [s_scr[q] for q in range(nquad)]
        for i, (c, q) in enumerate(units):
            lanes = slice(q * quad, (q + 1) * quad)
            y_scr[pl.ds(pl.multiple_of(base + c * CHUNK, CHUNK), CHUNK), lanes] = _dot_nt(qh[i], tile(st[q])) + yi[i]
            gam = gam_ref[0, pl.ds(gi * SCAN_GROUP + c, 1), lanes]
            st[q] = st[q] * gam + _dot(st[q], mp[i]) + nps[i]
            yield
        for q in range(nquad):
            s_scr[q] = st[q]

    assert rt_ref.shape[1] == SCAN_GROUP * CHUNK
    yield from group_stages(0)
    yield

    y = y_scr[...]
    bd = bd_ref[...]
    inv_n = 1.0 / HEAD_DIM
    mean = _dot(y, bd) * inv_n
    d = y - mean
    var = _dot(d * d, bd) * inv_n
    yn = d * lax.rsqrt(var + RWKV_GN_EPS) * lnw_ref[...] + lnb_ref[...]
    o_ref[0] = ((yn + bonus_ref[0]) * g_ref[0]).astype(o_ref.dtype)
    yield


def _rwkv_scan_specs(prep, ln_w, ln_b):
    rt = prep[0]
    bsz, seq, w = rt.shape
    tm = TOKEN_TILE
    nck = tm // CHUNK
    hid = jnp.arange(w) // HEAD_DIM
    bd = (hid[:, None] == hid[None, :]).astype(BF16)
    tok = pl.BlockSpec((1, tm, w), lambda b, j: (b, j, 0))
    ins = [tok] * 7 + [pl.BlockSpec((1, nck, w), lambda b, j: (b, j, 0)), tok, tok,
                       _resident((1, w)), _resident((1, w)), _resident((w, w))]
    args = (*prep, ln_w.reshape(1, w), ln_b.reshape(1, w), bd)
    scratch = [pltpu.VMEM((w // QUAD_LANES, HEAD_DIM, QUAD_LANES), F32), pltpu.VMEM((tm, w), F32)]
    return args, ins, tok, jax.ShapeDtypeStruct((bsz, seq, w), BF16), scratch


ATTN_PLANE = 128
ATTN_UNROLL = 2
ATTN_MAX_STRIDE = 4


def _attn_stages(q_ref, kp_ref, kc_ref, vp_ref, vc_ref, bias_ref, o_ref, l_ref,
                 q_scr, k_scr, v_scr, o_scr, l_scr, *hop_scr, dilation):
    blk = ATTN_BLK
    span = blk * dilation
    tile = q_ref.shape[1]
    nplane = q_ref.shape[2] // ATTN_PLANE
    first = pl.program_id(1) == 0
    hop = max(dilation // ATTN_MAX_STRIDE, 1)
    inner = dilation // hop
    for j in range(nplane):
        ls = slice(j * ATTN_PLANE, (j + 1) * ATTN_PLANE)
        q_scr[j] = q_ref[0, :, ls].astype(F32) * (HEAD_DIM ** -0.5)
        k_scr[j, 0:span] = kp_ref[0, :, ls].astype(F32)
        k_scr[j, span:] = kc_ref[0, :, ls].astype(F32)
        v_scr[j, 0:span] = vp_ref[0, :, ls].astype(F32)
        v_scr[j, span:] = vc_ref[0, :, ls].astype(F32)
    if hop > 1:
        assert tile == span
        for j in range(nplane):
            for b in range(hop):
                for src, dst in ((q_scr, hop_scr[0]), (k_scr, hop_scr[1]), (v_scr, hop_scr[2])):
                    n = src.shape[1] // hop
                    dst[j, b * n:(b + 1) * n] = src[j, pl.ds(b, n, stride=hop), :]
    q_src, k_src, v_src, o_dst, l_dst = ((q_scr, k_scr, v_scr, o_scr, l_scr) if hop == 1 else hop_scr)
    col = lax.broadcasted_iota(jnp.int32, (blk, 2 * blk), 1)
    lane = lax.broadcasted_iota(jnp.int32, (blk, ATTN_PLANE), 1)
    per_plane = ATTN_PLANE // HEAD_DIM
    mine = [(lane // HEAD_DIM) == hh for hh in range(per_plane)]
    heads = [(j, hh) for j in range(nplane) for hh in range(per_plane)]

    def rows_of(idx):
        if hop > 1:
            a, b = (idx % dilation) // hop, idx % hop
            return (pl.ds(b * (tile // hop) + a, blk, stride=inner),
                    pl.ds(b * ((span + tile) // hop) + a, 2 * blk, stride=inner))
        start = (idx % dilation) + (idx // dilation) * span
        if dilation == 1:
            start = pl.multiple_of(start, blk)
            return pl.ds(start, blk), pl.ds(start, 2 * blk)
        return pl.ds(start, blk, stride=dilation), pl.ds(start, 2 * blk, stride=dilation)

    def pair_stages(it):
        blocks = [it * ATTN_UNROLL + b for b in range(ATTN_UNROLL)]
        rows = [rows_of(idx) for idx in blocks]
        exists = [None if idx >= dilation else jnp.logical_or(jnp.logical_not(first), col >= blk)
                  for idx in blocks]
        qp = [[q_src[j, qr, :].astype(BF16) for j in range(nplane)] for qr, _ in rows]
        kp = [[k_src[j, kr, :].astype(BF16) for j in range(nplane)] for _, kr in rows]
        vp = [[v_src[j, kr, :].astype(BF16) for j in range(nplane)] for _, kr in rows]
        yield
        streams = [(b, i) for b in range(ATTN_UNROLL) for i in range(len(heads))]
        s = [_dot_nt(jnp.where(mine[heads[i][1]], qp[b][heads[i][0]], jnp.zeros_like(qp[b][heads[i][0]])),
                     kp[b][heads[i][0]]) for b, i in streams]
        yield
        s = [s[n] + bias_ref[i] for n, (b, i) in enumerate(streams)]
        s = [x if exists[b] is None else jnp.where(exists[b], x, MASK_VALUE) for x, (b, i) in zip(s, streams)]
        yield
        m = [jnp.max(x, axis=-1, keepdims=True) for x in s]
        yield
        pexp = [jnp.exp(x - mx) for x, mx in zip(s, m)]
        yield
        l = [jnp.sum(x, axis=-1, keepdims=True) for x in pexp]
        yield
        o = [_dot(pexp[n], vp[b][heads[i][0]]) / l[n] for n, (b, i) in enumerate(streams)]
        yield
        lse = [mx + jnp.log(lx) for mx, lx in zip(m, l)]
        yield
        for b in range(ATTN_UNROLL):
            for j in range(nplane):
                n0 = b * len(heads) + j * per_plane
                o_pair = o[n0]
                l_pair = jnp.broadcast_to(lse[n0], o_pair.shape)
                for hh in range(1, per_plane):
                    o_pair = jnp.where(mine[hh], o[n0 + hh], o_pair)
                    l_pair = jnp.where(mine[hh], lse[n0 + hh], l_pair)
                o_dst[j, rows[b][0], :] = o_pair
                l_dst[j, rows[b][0], :] = l_pair
        yield

    yield
    for it in range(tile // (blk * ATTN_UNROLL)):
        yield from pair_stages(it)
    for j in range(nplane):
        if hop > 1:
            n = tile // hop
            for b in range(hop):
                o_scr[j, pl.ds(b, n, stride=hop), :] = o_dst[j, b * n:(b + 1) * n]
                l_scr[j, pl.ds(b, n, stride=hop), :] = l_dst[j, b * n:(b + 1) * n]
        ls = slice(j * ATTN_PLANE, (j + 1) * ATTN_PLANE)
        o_ref[0, :, ls] = o_scr[j]
        l_ref[0, :, ls] = l_scr[j]
    yield


def _attn_kernel(*refs, dilation):
    for _ in _attn_stages(*refs, dilation=dilation):
        pass


SCAN_REFS = (13, 1, 2)
ATTN_REFS = (6, 2, 5)


def _scan_attn_kernel(*refs, dilations):
    counts = [SCAN_REFS] + [ATTN_REFS] * len(dilations)
    parts = [[] for _ in counts]
    pos = 0
    for kind in range(3):
        for part, cnt in zip(parts, counts):
            part.extend(refs[pos:pos + cnt[kind]])
            pos += cnt[kind]
    live = [_rwkv_scan_stages(*parts[0])]
    live += [_attn_stages(*part, dilation=dil) for part, dil in zip(parts[1:], dilations)]
    done = object()
    while live:
        live = [gen for gen in live if next(gen, done) is not done]


def _attn_bias(group, dilation):
    blk = ATTN_BLK
    heads = jnp.arange(group * HEADS_PER_GROUP + 1, (group + 1) * HEADS_PER_GROUP + 1, dtype=F32)
    slopes = jnp.exp2(-ALIBI_MAX_EXP * heads / ATTN_HEADS)
    steps = blk + jnp.arange(blk)[:, None] - jnp.arange(2 * blk)[None, :]
    in_window = (steps >= 0) & (steps <= blk)
    alibi = -slopes[:, None, None] * (steps * dilation).astype(F32)[None]
    return jnp.where(in_window[None], alibi, MASK_VALUE)


def _attn_specs(p_attn3, group, dilation):
    bsz, seq, cols = p_attn3.shape
    blk = ATTN_BLK
    gw = ATTN_OUT_WIDTH
    span = blk * dilation
    tile = max(span, TOKEN_TILE)
    nplane = gw // ATTN_PLANE
    per_kind = ATTN_WIDTH // gw

    def cur(kind):
        return pl.BlockSpec((1, tile, gw), lambda b, n: (b, n, kind * per_kind + group))

    def prev(kind):
        return pl.BlockSpec((1, span, gw),
                            lambda b, n: (b, jnp.maximum(n * (tile // span) - 1, 0), kind * per_kind + group))

    out_spec = pl.BlockSpec((1, tile, gw), lambda b, n: (b, n, 0))
    plane = lambda rows: pltpu.VMEM((nplane, rows, ATTN_PLANE), F32)
    args = (p_attn3, p_attn3, p_attn3, p_attn3, p_attn3, _attn_bias(group, dilation))
    ins = [cur(0), prev(1), cur(1), prev(2), cur(2), _resident((HEADS_PER_GROUP, blk, 2 * blk))]
    scratch = [plane(tile), plane(span + tile), plane(span + tile), plane(tile), plane(tile)]
    if dilation > ATTN_MAX_STRIDE:
        scratch = scratch + scratch
    return args, ins, [out_spec, out_spec], [jax.ShapeDtypeStruct((bsz, seq, gw), F32)] * 2, scratch, tile


def _attn_group(p_attn3, group, dilation):
    args, ins, outs, shapes, scratch, tile = _attn_specs(p_attn3, group, dilation)
    bsz, seq, _ = p_attn3.shape
    return pl.pallas_call(
        functools.partial(_attn_kernel, dilation=dilation),
        grid=(bsz, seq // tile),
        in_specs=ins, out_specs=outs, out_shape=shapes, scratch_shapes=scratch,
        compiler_params=_params(("parallel", "arbitrary")),
        name=f"attn_g{group}",
    )(*args)


def _rwkv_scan_attn(prep, ln_w, ln_b, p_attn3, groups):
    s_args, s_ins, s_out, s_shape, s_scratch = _rwkv_scan_specs(prep, ln_w, ln_b)
    args, ins, outs, shapes, scratch = list(s_args), list(s_ins), [s_out], [s_shape], list(s_scratch)
    for group in groups:
        a_args, a_ins, a_outs, a_shapes, a_scratch, tile = _attn_specs(p_attn3, group, ATTN_GROUPS[group][1])
        assert tile == TOKEN_TILE
        args += a_args
        ins += a_ins
        outs += a_outs
        shapes += a_shapes
        scratch += a_scratch
    bsz, seq, _ = p_attn3.shape
    res = pl.pallas_call(
        functools.partial(_scan_attn_kernel, dilations=tuple(ATTN_GROUPS[g][1] for g in groups)),
        grid=(bsz, seq // TOKEN_TILE),
        in_specs=ins, out_specs=outs, out_shape=shapes, scratch_shapes=scratch,
        compiler_params=_params(("parallel", "arbitrary")),
        name="rwkv_scan_attn",
    )(*args)
    return res[0], [(res[1 + 2 * i], res[2 + 2 * i]) for i in range(len(groups))]


def _merge_kernel(x_ref, ya_ref, pc_ref, pcprev_ref, o0_ref, o1_ref, o2_ref, l0_ref, l1_ref, l2_ref,
                  gate_ref, convw_ref, wb_ref, wout_ref, gain_ref, out_ref):
    first = pl.program_id(1) == 0
    cw = CONV_WIDTH
    pc = pc_ref[0].astype(F32)
    pprev = pcprev_ref[0].astype(F32)
    u = pc[:, cw:2 * cw] * pc[:, 2 * cw:]
    uprev = jnp.where(first, 0.0, pprev[:, cw:2 * cw] * pprev[:, 2 * cw:])
    cwt = convw_ref[...]
    yb = pc[:, :cw] * (cwt[0:1] * _shift_rows(u, uprev, 2) + cwt[1:2] * _shift_rows(u, uprev, 1) + cwt[2:3] * u)

    l0, l1, l2 = l0_ref[0], l1_ref[0], l2_ref[0]
    mx = jnp.maximum(jnp.maximum(l0, l1), l2)
    e0, e1, e2 = jnp.exp(l0 - mx), jnp.exp(l1 - mx), jnp.exp(l2 - mx)
    yc = (e0 * o0_ref[0] + e1 * o1_ref[0] + e2 * o2_ref[0]) / (e0 + e1 + e2)

    d = x_ref.shape[-1]
    gate = gate_ref[0].astype(F32)
    ra, rb = RWKV_WIDTH, RWKV_WIDTH + CONV_WIDTH
    merged = (_sigmoid(gate[:, :d]) * _dot(ya_ref[0], wb_ref[:ra])
              + _sigmoid(gate[:, d:2 * d]) * _dot(yb, wb_ref[ra:rb])
              + _sigmoid(gate[:, 2 * d:]) * _dot(yc, wb_ref[rb:]))
    out_ref[0] = x_ref[0] + _rms(_dot(merged, wout_ref[...]), gain_ref[...])


def _merge(x3, ya, p_conv3, attn, p_gate3, conv_w, wb_bf16, wout_bf16, gain, layer):
    bsz, seq, d = x3.shape
    tm = TOKEN_TILE
    tok = lambda width: pl.BlockSpec((1, tm, width), lambda b, j: (b, j, 0))
    (o0, l0), (o1, l1), (o2, l2) = attn
    gw = ATTN_OUT_WIDTH
    ins = [tok(d), tok(RWKV_WIDTH), tok(CONV_COLS),
           pl.BlockSpec((1, 8, CONV_COLS), lambda b, j: (b, jnp.maximum(j * (tm // 8) - 1, 0), 0)),
           tok(gw), tok(gw), tok(gw), tok(gw), tok(gw), tok(gw), tok(p_gate3.shape[-1]),
           _resident(conv_w.shape), _layer_resident(wb_bf16.shape, layer), _layer_resident(wout_bf16.shape, layer),
           _resident((1, d))]
    return pl.pallas_call(
        _merge_kernel,
        grid=(bsz, seq // tm),
        in_specs=ins, out_specs=tok(d),
        out_shape=jax.ShapeDtypeStruct((bsz, seq, d), F32),
        compiler_params=_params(("parallel", "parallel")),
        name="merge",
    )(x3, ya, p_conv3, p_conv3, o0, o1, o2, l0, l1, l2, p_gate3, conv_w, wb_bf16, wout_bf16, gain.reshape(1, d))


FFN_WEIGHT_CHUNKS = 8


def _load_cast_weight(w_hbm, layer, dst, stage, sems):
    ck = dst.shape[0] // FFN_WEIGHT_CHUNKS

    def copy(i):
        return pltpu.make_async_copy(w_hbm.at[layer, pl.ds(i * ck, ck)], stage.at[i % 2], sems.at[i % 2])

    copy(0).start()
    for i in range(FFN_WEIGHT_CHUNKS):
        if i + 1 < FFN_WEIGHT_CHUNKS:
            copy(i + 1).start()
        copy(i).wait()
        dst[i * ck:(i + 1) * ck] = stage[i % 2].astype(BF16)


def _ffn_kernel(x_ref, gpre_ref, gpost_ref, win_hbm, wout_hbm, out_ref, act_scr, win_ref, wout_ref,
                win_stage, wout_stage, win_sems, wout_sems, *, layer):
    @pl.when(pl.program_id(0) == 0)
    def _():
        _load_cast_weight(win_hbm, layer, win_ref, win_stage, win_sems)
        _load_cast_weight(wout_hbm, layer, wout_ref, wout_stage, wout_sems)

    x = x_ref[...]
    h = _rms(x, gpre_ref[...]).astype(BF16)
    dff = wout_ref.shape[0]
    for c in range(0, dff, MXU_TILE):
        gate = jnp.dot(h, win_ref[:, c:c + MXU_TILE], preferred_element_type=F32)
        up = jnp.dot(h, win_ref[:, dff + c:dff + c + MXU_TILE], preferred_element_type=F32)
        act_scr[:, c:c + MXU_TILE] = (gate * _sigmoid(gate) * up).astype(BF16)
    z = jnp.dot(act_scr[...], wout_ref[...], preferred_element_type=F32)
    out_ref[...] = x + _rms(z, gpost_ref[...])


def _ffn(x2d, gpre, gpost, win_f32, wout_f32, layer):
    m, d = x2d.shape
    tm = FFN_TILE
    _, dff, _ = wout_f32.shape
    cols_in = win_f32.shape[2]
    nck = FFN_WEIGHT_CHUNKS
    assert d % (16 * nck) == 0 and dff % (16 * nck) == 0
    return pl.pallas_call(
        functools.partial(_ffn_kernel, layer=layer),
        grid=(m // tm,),
        in_specs=[pl.BlockSpec((tm, d), lambda i: (i, 0)), _resident((1, d)), _resident((1, d)),
                  pl.BlockSpec(memory_space=pl.ANY), pl.BlockSpec(memory_space=pl.ANY)],
        out_specs=pl.BlockSpec((tm, d), lambda i: (i, 0)),
        out_shape=jax.ShapeDtypeStruct((m, d), F32),
        scratch_shapes=[pltpu.VMEM((tm, dff), BF16),
                        pltpu.VMEM((d, cols_in), BF16), pltpu.VMEM((dff, d), BF16),
                        pltpu.VMEM((2, d // nck, cols_in), F32), pltpu.VMEM((2, dff // nck, d), F32),
                        pltpu.SemaphoreType.DMA((2,)), pltpu.SemaphoreType.DMA((2,))],
        compiler_params=_params(("arbitrary",)),
        name="ffn",
    )(x2d, gpre.reshape(1, d), gpost.reshape(1, d), win_f32, wout_f32)


def _layer(layer, x3, big, norm_mix_pre, norm_mix_post, norm_ffn_pre, norm_ffn_post, rwkv_mu, rwkv_w0, rwkv_w_up,
           rwkv_a0, rwkv_a_up, rwkv_g_up, rwkv_k_k, rwkv_k_a, rwkv_r_k, rwkv_ln_w, rwkv_ln_b, conv_w):
    w_in, w_branch, w_out, w_ffn_in, w_ffn_out = big
    bsz, seq, d = x3.shape
    m = bsz * seq
    p_rwkv, p_conv, p_attn, p_gate = _inproj(x3.reshape(m, d), norm_mix_pre, w_in, layer)
    in3 = lambda t: t.reshape(bsz, seq, t.shape[-1])
    prep = _rwkv_prep(in3(p_rwkv), rwkv_mu, rwkv_w0, rwkv_w_up, rwkv_a0, rwkv_a_up, rwkv_g_up,
                      rwkv_k_k, rwkv_k_a, rwkv_r_k.reshape(-1))
    fused = [gi for gi, (_, dil) in enumerate(ATTN_GROUPS) if ATTN_BLK * dil <= TOKEN_TILE]
    ya, attn = _rwkv_scan_attn(prep, rwkv_ln_w, rwkv_ln_b, in3(p_attn), fused)
    attn += [_attn_group(in3(p_attn), gi, dil) for gi, (_, dil) in enumerate(ATTN_GROUPS) if gi not in fused]
    x3 = _merge(x3, ya, in3(p_conv), attn, in3(p_gate), conv_w, w_branch, w_out, norm_mix_post, layer)
    x2 = _ffn(x3.reshape(m, d), norm_ffn_pre, norm_ffn_post, w_ffn_in, w_ffn_out, layer)
    return x2.reshape(bsz, seq, d)


def kernel(x, norm_mix_pre, norm_mix_post, norm_ffn_pre, norm_ffn_post, w_in, rwkv_mu, rwkv_w0, rwkv_w_up, rwkv_a0, rwkv_a_up, rwkv_g_up, rwkv_k_k, rwkv_k_a, rwkv_r_k, rwkv_ln_w, rwkv_ln_b, conv_w, w_branch, w_out, w_ffn_in, w_ffn_out):
    big = tuple(w.astype(BF16) for w in (w_in, w_branch, w_out)) + (w_ffn_in, w_ffn_out)
    small = (norm_mix_pre, norm_mix_post, norm_ffn_pre, norm_ffn_post, rwkv_mu, rwkv_w0, rwkv_w_up,
             rwkv_a0, rwkv_a_up, rwkv_g_up, rwkv_k_k, rwkv_k_a, rwkv_r_k, rwkv_ln_w, rwkv_ln_b, conv_w)
    for layer in range(w_in.shape[0]):
        x = _layer(layer, x, big, *(p[layer] for p in small))
    return x
```
